```python
import jax, jax.numpy as jnp
from jax import lax
import numpy as np

D_MODEL = 1024
BATCH = 16
SEQ = 2048
DEPTH = 1
DEC_BATCH = 1
DEC_SEQ = 16384
PAST_LEN = 128

GRID_W = 64
ATTN_HEADS = 8
ATTN_KV_HEADS = 2
ATTN_HEAD_DIM = 64
RET_HEADS = 4
RET_KEY_DIM = 128
RET_VALUE_DIM = 256
ATTN_Q_W = ATTN_HEADS * ATTN_HEAD_DIM
ATTN_KV_W = ATTN_KV_HEADS * ATTN_HEAD_DIM
RET_QK_W = RET_HEADS * RET_KEY_DIM
RET_V_W = RET_HEADS * RET_VALUE_DIM
IN_PROJ_W = ATTN_Q_W + 2 * ATTN_KV_W + 2 * RET_QK_W + 2 * RET_V_W + 2 * D_MODEL
D_FF = -(-8 * D_MODEL // (3 * 256)) * 256
Q_BLOCK = 128
RET_CHUNK = 128
ROPE_THETA = 10000.0
EPS = 1e-6

kernel_name = "hybrid_gqa_axial_retention_encoder"


def rms_norm(x, gain):
    xf = x.astype(jnp.float32)
    y = xf * lax.rsqrt(jnp.mean(xf * xf, axis=-1, keepdims=True) + EPS)
    return (y * gain.astype(jnp.float32)).astype(x.dtype)


def axial_rope_tables(n_tokens, head_dim):
    n_rows = n_tokens // GRID_W
    row = jnp.repeat(jnp.arange(n_rows, dtype=jnp.float32), GRID_W)
    col = jnp.tile(jnp.arange(GRID_W, dtype=jnp.float32), n_rows)
    n_freq = head_dim // 4
    inv_freq = ROPE_THETA ** (-jnp.arange(n_freq, dtype=jnp.float32) / n_freq)
    ang = jnp.concatenate([row[:, None] * inv_freq, col[:, None] * inv_freq], axis=-1)
    return jnp.cos(ang), jnp.sin(ang)


def apply_rope(x, cos, sin):
    xf = x.astype(jnp.float32).reshape(x.shape[:-1] + (x.shape[-1] // 2, 2))
    c = cos[None, :, None, :]
    s = sin[None, :, None, :]
    x0, x1 = xf[..., 0], xf[..., 1]
    out = jnp.stack([x0 * c - x1 * s, x0 * s + x1 * c], axis=-1)
    return out.reshape(x.shape).astype(x.dtype)


def axial_gqa_attention(q, k, v, q_gain, k_gain):
    B, T = q.shape[0], q.shape[1]
    cos, sin = axial_rope_tables(T, ATTN_HEAD_DIM)
    q = apply_rope(rms_norm(q, q_gain), cos, sin)
    k = apply_rope(rms_norm(k, k_gain), cos, sin)
    group = ATTN_HEADS // ATTN_KV_HEADS
    scale = ATTN_HEAD_DIM ** -0.5
    n_blocks = T // Q_BLOCK
    q_blocks = q.reshape(B, n_blocks, Q_BLOCK, ATTN_KV_HEADS, group, ATTN_HEAD_DIM).transpose(1, 0, 2, 3, 4, 5)

    def one_block(qb):
        s = jnp.einsum('bqkgd,bskd->bkgqs', qb, k, preferred_element_type=jnp.float32) * scale
        p = jax.nn.softmax(s, axis=-1).astype(v.dtype)
        return jnp.einsum('bkgqs,bskd->bqkgd', p, v)

    out = lax.map(one_block, q_blocks)
    return out.transpose(1, 0, 2, 3, 4, 5).reshape(B, T, ATTN_Q_W)


def retention_one_direction(q, k, v, log_gamma, strict):
    B, T, H, dk = q.shape
    dv = v.shape[-1]
    C = RET_CHUNK
    N = T // C
    qc = q.reshape(B, N, C, H, dk)
    kc = k.reshape(B, N, C, H, dk)
    vc = v.reshape(B, N, C, H, dv)
    pos = jnp.arange(C, dtype=jnp.float32)
    diff = pos[:, None] - pos[None, :]
    mask = (diff > 0) if strict else (diff >= 0)
    decay_intra = jnp.where(mask[None], jnp.exp(log_gamma[:, None, None] * jnp.maximum(diff, 0.0)[None]), 0.0)
    scores = jnp.einsum('bnihd,bnjhd->bnhij', qc, kc) * decay_intra
    intra = jnp.einsum('bnhij,bnjhe->bnihe', scores, vc)
    k_decay = jnp.exp(log_gamma[None, :] * (C - 1 - pos)[:, None])
    chunk_kv = jnp.einsum('bnjhd,bnjhe->nbhde', kc * k_decay[:, :, None], vc)
    chunk_decay = jnp.exp(log_gamma * C)[:, None, None]

    def step(state, kv):
        return chunk_decay * state + kv, state

    _, prev_states = lax.scan(step, jnp.zeros((B, H, dk, dv), jnp.float32), chunk_kv)
    q_decay = jnp.exp(log_gamma[None, :] * (pos + 1.0)[:, None])
    cross = jnp.einsum('bnihd,nbhde->bnihe', qc * q_decay[:, :, None], prev_states)
    return (intra + cross).reshape(B, T, H, dv)


def bidirectional_retention(q, k, v, gate, decay_fwd, decay_bwd, norm_gain):
    B, T = q.shape[0], q.shape[1]
    cos, sin = axial_rope_tables(T, RET_KEY_DIM)
    qf = apply_rope(q, cos, sin).astype(jnp.float32) * (RET_KEY_DIM ** -0.5)
    kf = apply_rope(k, cos, sin).astype(jnp.float32)
    vf = v.astype(jnp.float32)
    fwd = retention_one_direction(qf, kf, vf, jax.nn.log_sigmoid(decay_fwd.astype(jnp.float32)), False)
    bwd = retention_one_direction(qf[:, ::-1], kf[:, ::-1], vf[:, ::-1],
                                  jax.nn.log_sigmoid(decay_bwd.astype(jnp.float32)), True)[:, ::-1]
    y = fwd + bwd
    mean = jnp.mean(y, axis=-1, keepdims=True)
    var = jnp.mean(jnp.square(y - mean), axis=-1, keepdims=True)
    y = ((y - mean) * lax.rsqrt(var + EPS)).reshape(B, T, RET_V_W) * norm_gain.astype(jnp.float32)
    return (jax.nn.silu(gate.astype(jnp.float32)) * y).astype(gate.dtype)


def encoder_layer(x, norm_mix, w_in, b_gate, q_norm, k_norm, ret_decay_fwd, ret_decay_bwd,
                  ret_norm, w_branch_attn, w_branch_ret, w_out, norm_ffn, w_ffn_in, w_ffn_out):
    B, T, _ = x.shape
    h = rms_norm(x, norm_mix)
    proj = h @ w_in
    widths = [ATTN_Q_W, ATTN_KV_W, ATTN_KV_W, RET_QK_W, RET_QK_W, RET_V_W, RET_V_W]
    q_a, k_a, v_a, q_r, k_r, v_r, g_r, gate_logits = jnp.split(proj, list(np.cumsum(widths)), axis=-1)
    attn = axial_gqa_attention(q_a.reshape(B, T, ATTN_HEADS, ATTN_HEAD_DIM),
                               k_a.reshape(B, T, ATTN_KV_HEADS, ATTN_HEAD_DIM),
                               v_a.reshape(B, T, ATTN_KV_HEADS, ATTN_HEAD_DIM), q_norm, k_norm)
    ret = bidirectional_retention(q_r.reshape(B, T, RET_HEADS, RET_KEY_DIM),
                                  k_r.reshape(B, T, RET_HEADS, RET_KEY_DIM),
                                  v_r.reshape(B, T, RET_HEADS, RET_VALUE_DIM),
                                  g_r, ret_decay_fwd, ret_decay_bwd, ret_norm)
    gates = jax.nn.sigmoid((gate_logits + b_gate).astype(jnp.float32)).astype(x.dtype)
    g_attn, g_ret = jnp.split(gates, 2, axis=-1)
    mixed = g_attn * (attn @ w_branch_attn) + g_ret * (ret @ w_branch_ret)
    x = x + mixed @ w_out
    h = rms_norm(x, norm_ffn)
    gt, up = jnp.split(h @ w_ffn_in, 2, axis=-1)
    return x + (jax.nn.silu(gt) * up) @ w_ffn_out


def run_trunk(x, norm_mix, w_in, b_gate, q_norm, k_norm, ret_decay_fwd, ret_decay_bwd, ret_norm,
              w_branch_attn, w_branch_ret, w_out, norm_ffn, w_ffn_in, w_ffn_out, norm_final):
    for l in range(DEPTH):
        x = encoder_layer(x, norm_mix[l], w_in[l], b_gate[l], q_norm[l], k_norm[l],
                          ret_decay_fwd[l], ret_decay_bwd[l], ret_norm[l], w_branch_attn[l],
                          w_branch_ret[l], w_out[l], norm_ffn[l], w_ffn_in[l], w_ffn_out[l])
    return rms_norm(x, norm_final)


def setup_inputs(seed: int = 0) -> dict:
    key = jax.random.key(seed)
    ks = jax.random.split(key, 20)
    f32 = jnp.float32

    def w(k, shape, fan_in):
        return jax.random.normal(k, shape, f32) * (fan_in ** -0.5)

    def gain(k, shape):
        return 1.0 + 0.02 * jax.random.normal(k, shape, f32)

    base_logit = jnp.log(jnp.exp2(5.0 + jnp.arange(RET_HEADS, dtype=f32)) - 1.0)
    return {
        "x_prompt": jax.random.normal(ks[0], (BATCH, SEQ, D_MODEL), f32),
        "x_sample": jax.random.normal(ks[1], (DEC_BATCH, DEC_SEQ, D_MODEL), f32),
        "norm_mix": gain(ks[2], (DEPTH, D_MODEL)),
        "w_in": w(ks[3], (DEPTH, D_MODEL, IN_PROJ_W), D_MODEL),
        "b_gate": 0.01 * jax.random.normal(ks[4], (DEPTH, 2 * D_MODEL), f32),
        "q_norm": gain(ks[5], (DEPTH, ATTN_HEAD_DIM)),
        "k_norm": gain(ks[6], (DEPTH, ATTN_HEAD_DIM)),
        "ret_decay_fwd": base_logit + 0.01 * jax.random.normal(ks[7], (DEPTH, RET_HEADS), f32),
        "ret_decay_bwd": base_logit + 0.01 * jax.random.normal(ks[8], (DEPTH, RET_HEADS), f32),
        "ret_norm": gain(ks[9], (DEPTH, RET_V_W)),
        "w_branch_attn": w(ks[10], (DEPTH, ATTN_Q_W, D_MODEL), ATTN_Q_W),
        "w_branch_ret": w(ks[11], (DEPTH, RET_V_W, D_MODEL), RET_V_W),
        "w_out": w(ks[12], (DEPTH, D_MODEL, D_MODEL), D_MODEL),
        "norm_ffn": gain(ks[13], (DEPTH, D_MODEL)),
        "w_ffn_in": w(ks[14], (DEPTH, D_MODEL, 2 * D_FF), D_MODEL),
        "w_ffn_out": w(ks[15], (DEPTH, D_FF, D_MODEL), D_FF),
        "norm_final": gain(ks[16], (D_MODEL,)),
    }


def reference(x_prompt, x_sample, norm_mix, w_in, b_gate, q_norm, k_norm, ret_decay_fwd, ret_decay_bwd,
              ret_norm, w_branch_attn, w_branch_ret, w_out, norm_ffn, w_ffn_in, w_ffn_out, norm_final):
    y_prompt = run_trunk(x_prompt, norm_mix, w_in, b_gate, q_norm, k_norm, ret_decay_fwd, ret_decay_bwd,
                         ret_norm, w_branch_attn, w_branch_ret, w_out, norm_ffn, w_ffn_in, w_ffn_out, norm_final)
    y_sample = run_trunk(x_sample, norm_mix, w_in, b_gate, q_norm, k_norm, ret_decay_fwd, ret_decay_bwd,
                         ret_norm, w_branch_attn, w_branch_ret, w_out, norm_ffn, w_ffn_in, w_ffn_out, norm_final)
    return (y_prompt, y_sample)
```

```python
import functools
import math

import jax
import jax.numpy as jnp
import numpy as np
from jax import lax
from jax.experimental import pallas as pl
from jax.experimental.pallas import tpu as pltpu

F32 = jnp.float32
BF16 = jnp.bfloat16

D_MODEL = 1024
GRID_W = 64
ATTN_HEADS = 8
ATTN_KV_HEADS = 2
ATTN_GROUP = ATTN_HEADS // ATTN_KV_HEADS
ATTN_HEAD_DIM = 64
RET_HEADS = 4
RET_KEY_DIM = 128
RET_VALUE_DIM = 256
ATTN_Q_W = ATTN_HEADS * ATTN_HEAD_DIM
ATTN_KV_W = ATTN_KV_HEADS * ATTN_HEAD_DIM
RET_QK_W = RET_HEADS * RET_KEY_DIM
RET_V_W = RET_HEADS * RET_VALUE_DIM
D_FF = 2816
ROPE_THETA = 10000.0
EPS = 1e-6

TOKEN_TILE = 512
RET_CHUNK = 128
FFN_CHUNKS = ((0, 1536), (1536, 2816))
ONES_ROWS = 16
V7X_VMEM_LIMIT = 56 * 1024 * 1024

NT_DIMS = (((1,), (1,)), ((), ()))
TN_DIMS = (((0,), (0,)), ((), ()))


def _const_spec(shape):
    nd = len(shape)
    return pl.BlockSpec(shape, lambda *_: (0,) * nd)


def _rms(x):
    return x * lax.rsqrt(jnp.mean(x * x, axis=-1, keepdims=True) + EPS)


def _in_proj_kernel(x_ref, gmix_ref, wt_ref, wn_ref, bg_ref, gq_ref, gk_ref,
                    cat_ref, sat_ref, ck_ref, sk_ref, cr_ref, sr_ref,
                    qt_ref, vt_ref, k_ref, qr_ref, kr_ref, vr_ref, gr_ref, gate_ref):
    tm = x_ref.shape[0]
    h = (_rms(x_ref[...]) * gmix_ref[...]).astype(BF16)

    t = lax.dot_general(wt_ref[...], h, NT_DIMS, preferred_element_type=F32)
    cat, sat = cat_ref[...], sat_ref[...]
    gq = gq_ref[...]
    half = ATTN_HEAD_DIM // 2
    for hd in range(ATTN_HEADS):
        r0 = hd * ATTN_HEAD_DIM
        blk = t[r0:r0 + ATTN_HEAD_DIM]
        inv = lax.rsqrt(jnp.mean(blk * blk, axis=0, keepdims=True) + EPS)
        xn = blk * inv * gq
        x0, x1 = xn[:half], xn[half:]
        qt_ref[0, r0:r0 + half, :] = (x0 * cat - x1 * sat).astype(BF16)
        qt_ref[0, r0 + half:r0 + ATTN_HEAD_DIM, :] = (x0 * sat + x1 * cat).astype(BF16)
    vt_ref[0] = t[ATTN_Q_W:ATTN_Q_W + ATTN_KV_W].astype(BF16)

    o = 0
    kf = jnp.dot(h, wn_ref[:, o:o + ATTN_KV_W], preferred_element_type=F32)
    o += ATTN_KV_W
    lane = lax.broadcasted_iota(jnp.int32, (tm, ATTN_KV_W), 1)
    lo = lane < ATTN_HEAD_DIM
    k2 = kf * kf
    s_lo = jnp.sum(jnp.where(lo, k2, 0.0), axis=-1, keepdims=True)
    s_hi = jnp.sum(jnp.where(lo, 0.0, k2), axis=-1, keepdims=True)
    inv = jnp.where(lo, lax.rsqrt(s_lo * (1.0 / ATTN_HEAD_DIM) + EPS),
                    lax.rsqrt(s_hi * (1.0 / ATTN_HEAD_DIM) + EPS))
    kn = kf * inv * gk_ref[...]
    first_half = (lane % ATTN_HEAD_DIM) < half
    partner = jnp.where(first_half, pltpu.roll(kn, ATTN_KV_W - half, 1), pltpu.roll(kn, half, 1))
    k_ref[...] = (kn * ck_ref[...] + partner * sk_ref[...]).astype(BF16)

    cr, sr = cr_ref[...], sr_ref[...]
    for dst, scale in ((qr_ref, RET_KEY_DIM ** -0.5), (kr_ref, None)):
        y = jnp.dot(h, wn_ref[:, o:o + RET_QK_W], preferred_element_type=F32)
        o += RET_QK_W
        for hd in range(RET_HEADS):
            c0 = hd * RET_KEY_DIM
            xh = y[:, c0:c0 + RET_KEY_DIM]
            r = xh * cr + pltpu.roll(xh, RET_KEY_DIM // 2, 1) * sr
            if scale is not None:
                r = r * scale
            dst[:, c0:c0 + RET_KEY_DIM] = r.astype(BF16)

    for dst in (vr_ref, gr_ref):
        dst[...] = jnp.dot(h, wn_ref[:, o:o + RET_V_W], preferred_element_type=F32).astype(BF16)
        o += RET_V_W

    for c in range(2):
        z = jnp.dot(h, wn_ref[:, o:o + D_MODEL], preferred_element_type=F32)
        z = z + bg_ref[:, c * D_MODEL:(c + 1) * D_MODEL]
        gate_ref[:, c * D_MODEL:(c + 1) * D_MODEL] = (1.0 / (1.0 + jnp.exp(-z))).astype(BF16)
        o += D_MODEL


def _in_proj(x2d, seq_len, p):
    n_tok = x2d.shape[0]
    tm = TOKEN_TILE
    n_tiles = n_tok // tm
    tiles_per_seq = seq_len // tm
    wn_cols = p["w_n"].shape[1]

    def tok(w):
        return pl.BlockSpec((tm, w), lambda i: (i, 0))

    def pos_rows(w):
        return pl.BlockSpec((tm, w), lambda i: (i % tiles_per_seq, 0))

    def pos_lanes(r):
        return pl.BlockSpec((r, tm), lambda i: (0, i % tiles_per_seq))

    out_shape = (
        jax.ShapeDtypeStruct((n_tiles, ATTN_Q_W, tm), BF16),
        jax.ShapeDtypeStruct((n_tiles, ATTN_KV_W, tm), BF16),
        jax.ShapeDtypeStruct((n_tok, ATTN_KV_W), BF16),
        jax.ShapeDtypeStruct((n_tok, RET_QK_W), BF16),
        jax.ShapeDtypeStruct((n_tok, RET_QK_W), BF16),
        jax.ShapeDtypeStruct((n_tok, RET_V_W), BF16),
        jax.ShapeDtypeStruct((n_tok, RET_V_W), BF16),
        jax.ShapeDtypeStruct((n_tok, 2 * D_MODEL), BF16),
    )
    out_specs = (
        pl.BlockSpec((1, ATTN_Q_W, tm), lambda i: (i, 0, 0)),
        pl.BlockSpec((1, ATTN_KV_W, tm), lambda i: (i, 0, 0)),
        tok(ATTN_KV_W), tok(RET_QK_W), tok(RET_QK_W), tok(RET_V_W), tok(RET_V_W), tok(2 * D_MODEL),
    )
    in_specs = [
        tok(D_MODEL),
        _const_spec((1, D_MODEL)),
        _const_spec((ATTN_Q_W + ATTN_KV_W, D_MODEL)),
        _const_spec((D_MODEL, wn_cols)),
        _const_spec((1, 2 * D_MODEL)),
        _const_spec((ATTN_HEAD_DIM, tm)),
        _const_spec((1, ATTN_KV_W)),
        pos_lanes(ATTN_HEAD_DIM // 2), pos_lanes(ATTN_HEAD_DIM // 2),
        pos_rows(ATTN_KV_W), pos_rows(ATTN_KV_W),
        pos_rows(RET_KEY_DIM), pos_rows(RET_KEY_DIM),
    ]
    return pl.pallas_call(
        _in_proj_kernel,
        grid=(n_tiles,),
        in_specs=in_specs,
        out_specs=out_specs,
        out_shape=out_shape,
        compiler_params=pltpu.CompilerParams(
            dimension_semantics=("arbitrary",), vmem_limit_bytes=V7X_VMEM_LIMIT),
        name="in_proj",
    )(x2d, p["g_mix"], p["w_t"], p["w_n"], p["b_gate"], p["gq"], p["gk"],
      p["cat"], p["sat"], p["ck"], p["sk"], p["cr"], p["sr"])


def _attn_kernel(qt_ref, k_ref, vt_ref, o_ref):
    tq = qt_ref.shape[2]
    n_kv, tk = vt_ref.shape[1], vt_ref.shape[3]
    zeros = jnp.zeros((ATTN_HEAD_DIM, tq), BF16)
    ones = jnp.ones((ONES_ROWS, tk), BF16)

    for hd in range(ATTN_HEADS):
        g = hd // ATTN_GROUP
        r0 = hd * ATTN_HEAD_DIM
        qh = qt_ref[0, r0:r0 + ATTN_HEAD_DIM, :]
        qpad = jnp.concatenate([qh, zeros] if g == 0 else [zeros, qh], axis=0)

        def body(j, carry, g=g, qpad=qpad):
            m, acc = carry
            start = pl.multiple_of(j * tk, tk)
            kb = k_ref[0, pl.ds(start, tk), :]
            s = jnp.dot(kb, qpad, preferred_element_type=F32)
            m_new = jnp.maximum(m, jnp.max(s, axis=0, keepdims=True))
            alpha = jnp.exp2(m - m_new)
            pt = jnp.exp2(s - m_new).astype(BF16)
            vb = vt_ref[0, j, g * ATTN_HEAD_DIM:(g + 1) * ATTN_HEAD_DIM, :]
            vext = jnp.concatenate([vb, ones], axis=0)
            pv = jnp.dot(vext, pt, preferred_element_type=F32)
            return m_new, alpha * acc + pv

        m0 = jnp.full((1, tq), -jnp.inf, F32)
        acc0 = jnp.zeros((ATTN_HEAD_DIM + ONES_ROWS, tq), F32)
        _, acc = lax.fori_loop(0, n_kv, body, (m0, acc0))
        denom = acc[ATTN_HEAD_DIM:ATTN_HEAD_DIM + 1, :]
        o_ref[0, r0:r0 + ATTN_HEAD_DIM, :] = (acc[:ATTN_HEAD_DIM] / denom).astype(BF16)


def _attention(qt, k, vt, batch, seq_len):
    tq = qt.shape[2]
    n_q = seq_len // tq
    k3 = k.reshape(batch, seq_len, ATTN_KV_W)
    vt4 = vt.reshape(batch, n_q, ATTN_KV_W, tq)
    return pl.pallas_call(
        _attn_kernel,
        grid=(batch, n_q),
        in_specs=[
            pl.BlockSpec((1, ATTN_Q_W, tq), lambda b, i: (b * n_q + i, 0, 0)),
            pl.BlockSpec((1, seq_len, ATTN_KV_W), lambda b, i: (b, 0, 0)),
            pl.BlockSpec((1, n_q, ATTN_KV_W, tq), lambda b, i: (b, 0, 0, 0)),
        ],
        out_specs=pl.BlockSpec((1, ATTN_Q_W, tq), lambda b, i: (b * n_q + i, 0, 0)),
        out_shape=jax.ShapeDtypeStruct(qt.shape, BF16),
        compiler_params=pltpu.CompilerParams(
            dimension_semantics=("arbitrary", "arbitrary"), vmem_limit_bytes=V7X_VMEM_LIMIT),
        name="attn",
    )(qt, k3, vt4)


def _ret_kernel(lg_ref, q_ref, k_ref, v_ref, g_ref, gain_ref, o_ref, sf_ref, sb_ref, sball_ref):
    hd = pl.program_id(1)
    phase = pl.program_id(2)
    i = pl.program_id(3)
    n_blk = pl.num_programs(3)
    tb = k_ref.shape[1]
    C = RET_CHUNK
    n_sub = tb // C
    lgf = lg_ref[0, hd]
    lgb = lg_ref[1, hd]

    row = lax.broadcasted_iota(jnp.int32, (C, RET_KEY_DIM), 0).astype(F32)

    @pl.when(phase == 0)
    def _backward_sweep():
        @pl.when(i == 0)
        def _():
            sb_ref[...] = jnp.zeros_like(sb_ref)

        blk = n_blk - 1 - i
        kdec = jnp.exp(lgb * row)
        cdec = jnp.exp(jnp.full((1, RET_VALUE_DIM), lgb * C, F32))
        for c in reversed(range(n_sub)):
            sl = slice(c * C, (c + 1) * C)
            state = sb_ref[...]
            sball_ref[blk * n_sub + c] = state.astype(BF16)
            kd = (k_ref[0, sl, :].astype(F32) * kdec).astype(BF16)
            kv = lax.dot_general(kd, v_ref[0, sl, :], TN_DIMS, preferred_element_type=F32)
            sb_ref[...] = cdec * state + kv

    @pl.when(phase == 1)
    def _forward_sweep():
        @pl.when(i == 0)
        def _():
            sf_ref[...] = jnp.zeros_like(sf_ref)

        col = lax.broadcasted_iota(jnp.int32, (C, C), 1).astype(F32)
        diff = row[:, :C] - col
        dmask = jnp.exp(jnp.where(diff >= 0, lgf * diff, -lgb * diff))
        qdec_f = jnp.exp(lgf * (row + 1.0))
        qdec_b = jnp.exp(lgb * (C - row))
        kdec = jnp.exp(lgf * (C - 1.0 - row))
        cdec = jnp.exp(jnp.full((1, RET_VALUE_DIM), lgf * C, F32))
        gain = gain_ref[...]
        for c in range(n_sub):
            sl = slice(c * C, (c + 1) * C)
            q = q_ref[0, sl, :]
            k = k_ref[0, sl, :]
            v = v_ref[0, sl, :]
            qf = q.astype(F32)
            state = sf_ref[...]
            scores = lax.dot_general(q, k, NT_DIMS, preferred_element_type=F32) * dmask
            lhs = jnp.concatenate(
                [scores.astype(BF16), (qf * qdec_f).astype(BF16), (qf * qdec_b).astype(BF16)], axis=1)
            rhs = jnp.concatenate([v, state.astype(BF16), sball_ref[i * n_sub + c]], axis=0)
            y = jnp.dot(lhs, rhs, preferred_element_type=F32)
            kd = (k.astype(F32) * kdec).astype(BF16)
            sf_ref[...] = cdec * state + lax.dot_general(kd, v, TN_DIMS, preferred_element_type=F32)

            mean = jnp.mean(y, axis=-1, keepdims=True)
            d = y - mean
            var = jnp.mean(d * d, axis=-1, keepdims=True)
            yn = d * lax.rsqrt(var + EPS) * gain
            gl = g_ref[0, sl, :].astype(F32)
            o_ref[0, sl, :] = (gl / (1.0 + jnp.exp(-gl)) * yn).astype(BF16)


def _retention(qr, kr, vr, gr, log_gamma, gain, batch, seq_len):
    tb = TOKEN_TILE
    n_blk = seq_len // tb
    n_chunks = seq_len // RET_CHUNK
    shp = lambda a: a.reshape(batch, seq_len, a.shape[-1])

    def both(b, h, p, i):
        return (b, jnp.where(p == 0, n_blk - 1 - i, i), h)

    def fwd_only(b, h, p, i):
        return (b, jnp.where(p == 0, 0, i), h)

    out = pl.pallas_call(
        _ret_kernel,
        grid=(batch, RET_HEADS, 2, n_blk),
        in_specs=[
            pl.BlockSpec(memory_space=pltpu.SMEM),
            pl.BlockSpec((1, tb, RET_KEY_DIM), fwd_only),
            pl.BlockSpec((1, tb, RET_KEY_DIM), both),
            pl.BlockSpec((1, tb, RET_VALUE_DIM), both),
            pl.BlockSpec((1, tb, RET_VALUE_DIM), fwd_only),
            pl.BlockSpec((1, RET_VALUE_DIM), lambda b, h, p, i: (0, h)),
        ],
        out_specs=pl.BlockSpec((1, tb, RET_VALUE_DIM), fwd_only),
        out_shape=jax.ShapeDtypeStruct((batch, seq_len, RET_V_W), BF16),
        scratch_shapes=[
            pltpu.VMEM((RET_KEY_DIM, RET_VALUE_DIM), F32),
            pltpu.VMEM((RET_KEY_DIM, RET_VALUE_DIM), F32),
            pltpu.VMEM((n_chunks, RET_KEY_DIM, RET_VALUE_DIM), BF16),
        ],
        compiler_params=pltpu.CompilerParams(
            dimension_semantics=("arbitrary",) * 4, vmem_limit_bytes=V7X_VMEM_LIMIT),
        name="retention",
    )(log_gamma, shp(qr), shp(kr), shp(vr), shp(gr), gain)
    return out.reshape(batch * seq_len, RET_V_W)


def _mix_kernel(x_ref, at_ref, ret_ref, gate_ref, wa_ref, wr_ref, wo_ref, o_ref):
    a = lax.dot_general(at_ref[0], wa_ref[...], TN_DIMS, preferred_element_type=F32)
    r = jnp.dot(ret_ref[...], wr_ref[...], preferred_element_type=F32)
    ga = gate_ref[:, :D_MODEL].astype(F32)
    gr = gate_ref[:, D_MODEL:].astype(F32)
    mixed = (ga * a + gr * r).astype(BF16)
    o_ref[...] = x_ref[...] + jnp.dot(mixed, wo_ref[...], preferred_element_type=F32)


def _mix(x2d, attn_t, ret, gates, p):
    n_tok = x2d.shape[0]
    tm = TOKEN_TILE
    tok = lambda w: pl.BlockSpec((tm, w), lambda i: (i, 0))
    return pl.pallas_call(
        _mix_kernel,
        grid=(n_tok // tm,),
        in_specs=[
            tok(D_MODEL),
            pl.BlockSpec((1, ATTN_Q_W, tm), lambda i: (i, 0, 0)),
            tok(RET_V_W), tok(2 * D_MODEL),
            _const_spec((ATTN_Q_W, D_MODEL)),
            _const_spec((RET_V_W, D_MODEL)),
            _const_spec((D_MODEL, D_MODEL)),
        ],
        out_specs=tok(D_MODEL),
        out_shape=jax.ShapeDtypeStruct((n_tok, D_MODEL), F32),
        compiler_params=pltpu.CompilerParams(
            dimension_semantics=("arbitrary",), vmem_limit_bytes=V7X_VMEM_LIMIT),
        name="mix",
    )(x2d, attn_t, ret, gates, p["w_a"], p["w_r"], p["w_o"])


def _ffn_kernel(x_ref, gffn_ref, win_ref, wout_ref, gfin_ref, o_ref):
    x = x_ref[...]
    h = (_rms(x) * gffn_ref[...]).astype(BF16)
    acc = x
    for c0, c1 in FFN_CHUNKS:
        gt = jnp.dot(h, win_ref[:, c0:c1], preferred_element_type=F32)
        up = jnp.dot(h, win_ref[:, D_FF + c0:D_FF + c1], preferred_element_type=F32)
        act = (gt / (1.0 + jnp.exp(-gt)) * up).astype(BF16)
        acc = acc + jnp.dot(act, wout_ref[c0:c1, :], preferred_element_type=F32)
    o_ref[...] = _rms(acc) * gfin_ref[...]


def _ffn(x2d, p):
    n_tok = x2d.shape[0]
    tm = TOKEN_TILE
    tok = pl.BlockSpec((tm, D_MODEL), lambda i: (i, 0))
    return pl.pallas_call(
        _ffn_kernel,
        grid=(n_tok // tm,),
        in_specs=[
            tok,
            _const_spec((1, D_MODEL)),
            _const_spec((D_MODEL, 2 * D_FF)),
            _const_spec((D_FF, D_MODEL)),
            _const_spec((1, D_MODEL)),
        ],
        out_specs=tok,
        out_shape=jax.ShapeDtypeStruct((n_tok, D_MODEL), F32),
        compiler_params=pltpu.CompilerParams(
            dimension_semantics=("arbitrary",), vmem_limit_bytes=V7X_VMEM_LIMIT),
        name="ffn",
    )(x2d, p["g_ffn"], p["w_ffn_in"], p["w_ffn_out"], p["g_fin"])


def _deinterleave(n_heads, head_dim):
    within = np.concatenate([np.arange(0, head_dim, 2), np.arange(1, head_dim, 2)])
    return (np.arange(n_heads)[:, None] * head_dim + within[None, :]).reshape(-1)


def _prepare_params(norm_mix, w_in, b_gate, q_norm, k_norm, ret_decay_fwd, ret_decay_bwd, ret_norm,
                    w_branch_attn, w_branch_ret, w_out, norm_ffn, w_ffn_in, w_ffn_out, norm_final):
    widths = [ATTN_Q_W, ATTN_KV_W, ATTN_KV_W, RET_QK_W, RET_QK_W, RET_V_W, RET_V_W, 2 * D_MODEL]
    offs = np.concatenate([[0], np.cumsum(widths)])
    seg = [w_in[:, offs[j]:offs[j + 1]] for j in range(len(widths))]
    w_qa, w_ka, w_va, w_qr, w_kr, w_vr, w_gr, w_gate = seg
    w_qa = w_qa[:, _deinterleave(ATTN_HEADS, ATTN_HEAD_DIM)]
    w_ka = w_ka[:, _deinterleave(ATTN_KV_HEADS, ATTN_HEAD_DIM)]
    w_qr = w_qr[:, _deinterleave(RET_HEADS, RET_KEY_DIM)]
    w_kr = w_kr[:, _deinterleave(RET_HEADS, RET_KEY_DIM)]
    head_perm = _deinterleave(1, ATTN_HEAD_DIM)
    q_scale = ATTN_HEAD_DIM ** -0.5 * math.log2(math.e)
    gq = (q_norm[head_perm] * q_scale).astype(F32)
    return {
        "g_mix": norm_mix.reshape(1, D_MODEL),
        "w_t": jnp.concatenate([w_qa, w_va], axis=1).T.astype(BF16),
        "w_n": jnp.concatenate([w_ka, w_qr, w_kr, w_vr, w_gr, w_gate], axis=1).astype(BF16),
        "b_gate": b_gate.reshape(1, 2 * D_MODEL),
        "gq": jnp.broadcast_to(gq[:, None], (ATTN_HEAD_DIM, TOKEN_TILE)),
        "gk": jnp.tile(k_norm[head_perm], ATTN_KV_HEADS).reshape(1, ATTN_KV_W),
        "log_gamma": jnp.stack([jax.nn.log_sigmoid(ret_decay_fwd.astype(F32)),
                                jax.nn.log_sigmoid(ret_decay_bwd.astype(F32))]),
        "ret_gain": ret_norm.reshape(1, RET_V_W),
        "w_a": w_branch_attn.astype(BF16),
        "w_r": w_branch_ret.astype(BF16),
        "w_o": w_out.astype(BF16),
        "g_ffn": norm_ffn.reshape(1, D_MODEL),
        "w_ffn_in": w_ffn_in.astype(BF16),
        "w_ffn_out": w_ffn_out.astype(BF16),
        "g_fin": norm_final.reshape(1, D_MODEL),
    }


def _rope_tables(seq_len):
    n_rows = seq_len // GRID_W
    row = jnp.repeat(jnp.arange(n_rows, dtype=F32), GRID_W)
    col = jnp.tile(jnp.arange(GRID_W, dtype=F32), n_rows)

    def cos_sin(head_dim):
        n_freq = head_dim // 4
        inv_freq = ROPE_THETA ** (-jnp.arange(n_freq, dtype=F32) / n_freq)
        ang = jnp.concatenate([row[:, None] * inv_freq, col[:, None] * inv_freq], axis=-1)
        return jnp.cos(ang), jnp.sin(ang)

    ca, sa = cos_sin(ATTN_HEAD_DIM)
    cr, sr = cos_sin(RET_KEY_DIM)
    return {
        "cat": ca.T, "sat": sa.T,
        "ck": jnp.tile(ca, (1, 2 * ATTN_KV_HEADS)),
        "sk": jnp.tile(jnp.concatenate([-sa, sa], axis=-1), (1, ATTN_KV_HEADS)),
        "cr": jnp.concatenate([cr, cr], axis=-1),
        "sr": jnp.concatenate([-sr, sr], axis=-1),
    }


def _trunk(x, params):
    batch, seq_len, _ = x.shape
    p = dict(params, **_rope_tables(seq_len))
    x2d = x.reshape(batch * seq_len, D_MODEL)
    qt, vt, k, qr, kr, vr, gr, gates = _in_proj(x2d, seq_len, p)
    attn_t = _attention(qt, k, vt, batch, seq_len)
    ret = _retention(qr, kr, vr, gr, p["log_gamma"], p["ret_gain"], batch, seq_len)
    x1 = _mix(x2d, attn_t, ret, gates, p)
    return _ffn(x1, p).reshape(batch, seq_len, D_MODEL)


def kernel(x_prompt, x_sample, norm_mix, w_in, b_gate, q_norm, k_norm, ret_decay_fwd, ret_decay_bwd,
           ret_norm, w_branch_attn, w_branch_ret, w_out, norm_ffn, w_ffn_in, w_ffn_out, norm_final):
    params = _prepare_params(norm_mix[0], w_in[0], b_gate[0], q_norm[0], k_norm[0], ret_decay_fwd[0],
                             ret_decay_bwd[0], ret_norm[0], w_branch_attn[0], w_branch_ret[0], w_out[0],
                             norm_ffn[0], w_ffn_in[0], w_ffn_out[0], norm_final)
    return _trunk(x_prompt, params), _trunk(x_sample, params)
```

```python
import functools
import math

import jax
import jax.numpy as jnp
import numpy as np
from jax import lax
from jax.experimental import pallas as pl
from jax.experimental.pallas import tpu as pltpu

F32 = jnp.float32
BF16 = jnp.bfloat16

D_MODEL = 1024
GRID_W = 64
ATTN_HEADS = 8
ATTN_KV_HEADS = 2
ATTN_GROUP = ATTN_HEADS // ATTN_KV_HEADS
ATTN_HEAD_DIM = 64
RET_HEADS = 4
RET_KEY_DIM = 128
RET_VALUE_DIM = 256
ATTN_Q_W = ATTN_HEADS * ATTN_HEAD_DIM
ATTN_KV_W = ATTN_KV_HEADS * ATTN_HEAD_DIM
RET_QK_W = RET_HEADS * RET_KEY_DIM
RET_V_W = RET_HEADS * RET_VALUE_DIM
D_FF = 2816
ROPE_THETA = 10000.0
EPS = 1e-6

TOKEN_TILE = 512
RET_CHUNK = 128
FFN_CHUNKS = ((0, 1536), (1536, 2816))
ONES_ROWS = 16
V7X_VMEM_LIMIT = 56 * 1024 * 1024

NT_DIMS = (((1,), (1,)), ((), ()))
TN_DIMS = (((0,), (0,)), ((), ()))


def _const_spec(shape):
    nd = len(shape)
    return pl.BlockSpec(shape, lambda *_: (0,) * nd)


def _rms(x):
    return x * lax.rsqrt(jnp.mean(x * x, axis=-1, keepdims=True) + EPS)


def _in_proj_kernel(x_ref, gmix_ref, wt_ref, wn_ref, bg_ref, gq_ref, gk_ref,
                    cat_ref, sat_ref, ck_ref, sk_ref, cr_ref, sr_ref,
                    qt_ref, vt_ref, k_ref, qr_ref, kr_ref, vr_ref, gr_ref, gate_ref):
    tm = x_ref.shape[0]
    h = (_rms(x_ref[...]) * gmix_ref[...]).astype(BF16)

    t = lax.dot_general(wt_ref[...], h, NT_DIMS, preferred_element_type=F32)
    cat, sat = cat_ref[...], sat_ref[...]
    gq = gq_ref[...]
    half = ATTN_HEAD_DIM // 2
    for hd in range(ATTN_HEADS):
        r0 = hd * ATTN_HEAD_DIM
        blk = t[r0:r0 + ATTN_HEAD_DIM]
        inv = lax.rsqrt(jnp.mean(blk * blk, axis=0, keepdims=True) + EPS)
        xn = blk * inv * gq
        x0, x1 = xn[:half], xn[half:]
        qt_ref[0, r0:r0 + half, :] = (x0 * cat - x1 * sat).astype(BF16)
        qt_ref[0, r0 + half:r0 + ATTN_HEAD_DIM, :] = (x0 * sat + x1 * cat).astype(BF16)
    vt_ref[0] = t[ATTN_Q_W:ATTN_Q_W + ATTN_KV_W].astype(BF16)

    o = 0
    kf = jnp.dot(h, wn_ref[:, o:o + ATTN_KV_W], preferred_element_type=F32)
    o += ATTN_KV_W
    lane = lax.broadcasted_iota(jnp.int32, (tm, ATTN_KV_W), 1)
    lo = lane < ATTN_HEAD_DIM
    k2 = kf * kf
    s_lo = jnp.sum(jnp.where(lo, k2, 0.0), axis=-1, keepdims=True)
    s_hi = jnp.sum(jnp.where(lo, 0.0, k2), axis=-1, keepdims=True)
    inv = jnp.where(lo, lax.rsqrt(s_lo * (1.0 / ATTN_HEAD_DIM) + EPS),
                    lax.rsqrt(s_hi * (1.0 / ATTN_HEAD_DIM) + EPS))
    kn = kf * inv * gk_ref[...]
    first_half = (lane % ATTN_HEAD_DIM) < half
    partner = jnp.where(first_half, pltpu.roll(kn, ATTN_KV_W - half, 1), pltpu.roll(kn, half, 1))
    k_ref[...] = (kn * ck_ref[...] + partner * sk_ref[...]).astype(BF16)

    cr, sr = cr_ref[...], sr_ref[...]
    for dst, scale in ((qr_ref, RET_KEY_DIM ** -0.5), (kr_ref, None)):
        y = jnp.dot(h, wn_ref[:, o:o + RET_QK_W], preferred_element_type=F32)
        o += RET_QK_W
        for hd in range(RET_HEADS):
            c0 = hd * RET_KEY_DIM
            xh = y[:, c0:c0 + RET_KEY_DIM]
            r = xh * cr + pltpu.roll(xh, RET_KEY_DIM // 2, 1) * sr
            if scale is not None:
                r = r * scale
            dst[:, c0:c0 + RET_KEY_DIM] = r.astype(BF16)

    for dst in (vr_ref, gr_ref):
        dst[...] = jnp.dot(h, wn_ref[:, o:o + RET_V_W], preferred_element_type=F32).astype(BF16)
        o += RET_V_W

    for c in range(2):
        z = jnp.dot(h, wn_ref[:, o:o + D_MODEL], preferred_element_type=F32)
        z = z + bg_ref[:, c * D_MODEL:(c + 1) * D_MODEL]
        gate_ref[:, c * D_MODEL:(c + 1) * D_MODEL] = (1.0 / (1.0 + jnp.exp(-z))).astype(BF16)
        o += D_MODEL


def _in_proj(x2d, seq_len, p):
    n_tok = x2d.shape[0]
    tm = TOKEN_TILE
    n_tiles = n_tok // tm
    tiles_per_seq = seq_len // tm
    wn_cols = p["w_n"].shape[1]

    def tok(w):
        return pl.BlockSpec((tm, w), lambda i: (i, 0))

    def pos_rows(w):
        return pl.BlockSpec((tm, w), lambda i: (i % tiles_per_seq, 0))

    def pos_lanes(r):
        return pl.BlockSpec((r, tm), lambda i: (0, i % tiles_per_seq))

    out_shape = (
        jax.ShapeDtypeStruct((n_tiles, ATTN_Q_W, tm), BF16),
        jax.ShapeDtypeStruct((n_tiles, ATTN_KV_W, tm), BF16),
        jax.ShapeDtypeStruct((n_tok, ATTN_KV_W), BF16),
        jax.ShapeDtypeStruct((n_tok, RET_QK_W), BF16),
        jax.ShapeDtypeStruct((n_tok, RET_QK_W), BF16),
        jax.ShapeDtypeStruct((n_tok, RET_V_W), BF16),
        jax.ShapeDtypeStruct((n_tok, RET_V_W), BF16),
        jax.ShapeDtypeStruct((n_tok, 2 * D_MODEL), BF16),
    )
    out_specs = (
        pl.BlockSpec((1, ATTN_Q_W, tm), lambda i: (i, 0, 0)),
        pl.BlockSpec((1, ATTN_KV_W, tm), lambda i: (i, 0, 0)),
        tok(ATTN_KV_W), tok(RET_QK_W), tok(RET_QK_W), tok(RET_V_W), tok(RET_V_W), tok(2 * D_MODEL),
    )
    in_specs = [
        tok(D_MODEL),
        _const_spec((1, D_MODEL)),
        _const_spec((ATTN_Q_W + ATTN_KV_W, D_MODEL)),
        _const_spec((D_MODEL, wn_cols)),
        _const_spec((1, 2 * D_MODEL)),
        _const_spec((ATTN_HEAD_DIM, tm)),
        _const_spec((1, ATTN_KV_W)),
        pos_lanes(ATTN_HEAD_DIM // 2), pos_lanes(ATTN_HEAD_DIM // 2),
        pos_rows(ATTN_KV_W), pos_rows(ATTN_KV_W),
        pos_rows(RET_KEY_DIM), pos_rows(RET_KEY_DIM),
    ]
    return pl.pallas_call(
        _in_proj_kernel,
        grid=(n_tiles,),
        in_specs=in_specs,
        out_specs=out_specs,
        out_shape=out_shape,
        compiler_params=pltpu.CompilerParams(
            dimension_semantics=("arbitrary",), vmem_limit_bytes=V7X_VMEM_LIMIT),
        name="in_proj",
    )(x2d, p["g_mix"], p["w_t"], p["w_n"], p["b_gate"], p["gq"], p["gk"],
      p["cat"], p["sat"], p["ck"], p["sk"], p["cr"], p["sr"])


def _attn_kernel(qt_ref, k_ref, vt_ref, o_ref, qpad_ref, m_ref, acc_ref):
    tq = qt_ref.shape[2]
    n_kv, tk = vt_ref.shape[1], vt_ref.shape[3]
    zeros = jnp.zeros((ATTN_HEAD_DIM, tq), BF16)
    ones = jnp.ones((ONES_ROWS, tk), BF16)

    for hd in range(ATTN_HEADS):
        r0 = hd * ATTN_HEAD_DIM
        qh = qt_ref[0, r0:r0 + ATTN_HEAD_DIM, :]
        qpad_ref[hd] = jnp.concatenate([qh, zeros] if hd < ATTN_GROUP else [zeros, qh], axis=0)
    m_ref[...] = jnp.full(m_ref.shape, -jnp.inf, F32)
    acc_ref[...] = jnp.zeros(acc_ref.shape, F32)

    def body(j, carry):
        start = pl.multiple_of(j * tk, tk)
        kb = k_ref[0, pl.ds(start, tk), :]
        vext = [jnp.concatenate([vt_ref[0, j, g * ATTN_HEAD_DIM:(g + 1) * ATTN_HEAD_DIM, :], ones], axis=0)
                for g in range(ATTN_KV_HEADS)]
        scores = [jnp.dot(kb, qpad_ref[hd], preferred_element_type=F32)
                  for hd in range(ATTN_HEADS)]
        probs, alphas = [], []
        for hd in range(ATTN_HEADS):
            m = m_ref[hd]
            m_new = jnp.maximum(m, jnp.max(scores[hd], axis=0, keepdims=True))
            m_ref[hd] = m_new
            alphas.append(jnp.exp2(m - m_new))
            probs.append(jnp.exp2(scores[hd] - m_new).astype(BF16))
        for hd in range(ATTN_HEADS):
            pv = jnp.dot(vext[hd // ATTN_GROUP], probs[hd], preferred_element_type=F32)
            acc_ref[hd] = alphas[hd] * acc_ref[hd] + pv
        return carry

    lax.fori_loop(0, n_kv, body, 0)
    for hd in range(ATTN_HEADS):
        r0 = hd * ATTN_HEAD_DIM
        acc = acc_ref[hd]
        denom = acc[ATTN_HEAD_DIM:ATTN_HEAD_DIM + 1, :]
        o_ref[0, r0:r0 + ATTN_HEAD_DIM, :] = (acc[:ATTN_HEAD_DIM] / denom).astype(BF16)


def _attention(qt, k, vt, batch, seq_len):
    tq = qt.shape[2]
    n_q = seq_len // tq
    k3 = k.reshape(batch, seq_len, ATTN_KV_W)
    vt4 = vt.reshape(batch, n_q, ATTN_KV_W, tq)
    return pl.pallas_call(
        _attn_kernel,
        grid=(batch, n_q),
        in_specs=[
            pl.BlockSpec((1, ATTN_Q_W, tq), lambda b, i: (b * n_q + i, 0, 0)),
            pl.BlockSpec((1, seq_len, ATTN_KV_W), lambda b, i: (b, 0, 0)),
            pl.BlockSpec((1, n_q, ATTN_KV_W, tq), lambda b, i: (b, 0, 0, 0)),
        ],
        out_specs=pl.BlockSpec((1, ATTN_Q_W, tq), lambda b, i: (b * n_q + i, 0, 0)),
        out_shape=jax.ShapeDtypeStruct(qt.shape, BF16),
        scratch_shapes=[
            pltpu.VMEM((ATTN_HEADS, 2 * ATTN_HEAD_DIM, tq), BF16),
            pltpu.VMEM((ATTN_HEADS, 1, tq), F32),
            pltpu.VMEM((ATTN_HEADS, ATTN_HEAD_DIM + ONES_ROWS, tq), F32),
        ],
        compiler_params=pltpu.CompilerParams(
            dimension_semantics=("arbitrary", "arbitrary"), vmem_limit_bytes=V7X_VMEM_LIMIT),
        name="attn",
    )(qt, k3, vt4)


def _ret_kernel(lg_ref, q_ref, k_ref, v_ref, g_ref, gain_ref, o_ref, sf_ref, sb_ref, sball_ref):
    hd = pl.program_id(1)
    phase = pl.program_id(2)
    i = pl.program_id(3)
    n_blk = pl.num_programs(3)
    tb = k_ref.shape[1]
    C = RET_CHUNK
    n_sub = tb // C
    lgf = lg_ref[0, hd]
    lgb = lg_ref[1, hd]

    row = lax.broadcasted_iota(jnp.int32, (C, RET_KEY_DIM), 0).astype(F32)

    @pl.when(phase == 0)
    def _backward_sweep():
        @pl.when(i == 0)
        def _():
            sb_ref[...] = jnp.zeros_like(sb_ref)

        blk = n_blk - 1 - i
        kdec = jnp.exp(lgb * row)
        cdec = jnp.exp(jnp.full((1, RET_VALUE_DIM), lgb * C, F32))
        for c in reversed(range(n_sub)):
            sl = slice(c * C, (c + 1) * C)
            state = sb_ref[...]
            sball_ref[blk * n_sub + c] = state.astype(BF16)
            kd = (k_ref[0, sl, :].astype(F32) * kdec).astype(BF16)
            kv = lax.dot_general(kd, v_ref[0, sl, :], TN_DIMS, preferred_element_type=F32)
            sb_ref[...] = cdec * state + kv

    @pl.when(phase == 1)
    def _forward_sweep():
        @pl.when(i == 0)
        def _():
            sf_ref[...] = jnp.zeros_like(sf_ref)

        col = lax.broadcasted_iota(jnp.int32, (C, C), 1).astype(F32)
        diff = row[:, :C] - col
        dmask = jnp.exp(jnp.where(diff >= 0, lgf * diff, -lgb * diff))
        qdec_f = jnp.exp(lgf * (row + 1.0))
        qdec_b = jnp.exp(lgb * (C - row))
        kdec = jnp.exp(lgf * (C - 1.0 - row))
        cdec = jnp.exp(jnp.full((1, RET_VALUE_DIM), lgf * C, F32))
        gain = gain_ref[...]
        for c in range(n_sub):
            sl = slice(c * C, (c + 1) * C)
            q = q_ref[0, sl, :]
            k = k_ref[0, sl, :]
            v = v_ref[0, sl, :]
            qf = q.astype(F32)
            state = sf_ref[...]
            scores = lax.dot_general(q, k, NT_DIMS, preferred_element_type=F32) * dmask
            lhs = jnp.concatenate(
                [scores.astype(BF16), (qf * qdec_f).astype(BF16), (qf * qdec_b).astype(BF16)], axis=1)
            rhs = jnp.concatenate([v, state.astype(BF16), sball_ref[i * n_sub + c]], axis=0)
            y = jnp.dot(lhs, rhs, preferred_element_type=F32)
            kd = (k.astype(F32) * kdec).astype(BF16)
            sf_ref[...] = cdec * state + lax.dot_general(kd, v, TN_DIMS, preferred_element_type=F32)

            mean = jnp.mean(y, axis=-1, keepdims=True)
            d = y - mean
            var = jnp.mean(d * d, axis=-1, keepdims=True)
            yn = d * lax.rsqrt(var + EPS) * gain
            gl = g_ref[0, sl, :].astype(F32)
            o_ref[0, sl, :] = (gl / (1.0 + jnp.exp(-gl)) * yn).astype(BF16)


def _retention(qr, kr, vr, gr, log_gamma, gain, batch, seq_len):
    tb = TOKEN_TILE
    n_blk = seq_len // tb
    n_chunks = seq_len // RET_CHUNK
    shp = lambda a: a.reshape(batch, seq_len, a.shape[-1])

    def both(b, h, p, i):
        return (b, jnp.where(p == 0, n_blk - 1 - i, i), h)

    def fwd_only(b, h, p, i):
        return (b, jnp.where(p == 0, 0, i), h)

    out = pl.pallas_call(
        _ret_kernel,
        grid=(batch, RET_HEADS, 2, n_blk),
        in_specs=[
            pl.BlockSpec(memory_space=pltpu.SMEM),
            pl.BlockSpec((1, tb, RET_KEY_DIM), fwd_only),
            pl.BlockSpec((1, tb, RET_KEY_DIM), both),
            pl.BlockSpec((1, tb, RET_VALUE_DIM), both),
            pl.BlockSpec((1, tb, RET_VALUE_DIM), fwd_only),
            pl.BlockSpec((1, RET_VALUE_DIM), lambda b, h, p, i: (0, h)),
        ],
        out_specs=pl.BlockSpec((1, tb, RET_VALUE_DIM), fwd_only),
        out_shape=jax.ShapeDtypeStruct((batch, seq_len, RET_V_W), BF16),
        scratch_shapes=[
            pltpu.VMEM((RET_KEY_DIM, RET_VALUE_DIM), F32),
            pltpu.VMEM((RET_KEY_DIM, RET_VALUE_DIM), F32),
            pltpu.VMEM((n_chunks, RET_KEY_DIM, RET_VALUE_DIM), BF16),
        ],
        compiler_params=pltpu.CompilerParams(
            dimension_semantics=("arbitrary",) * 4, vmem_limit_bytes=V7X_VMEM_LIMIT),
        name="retention",
    )(log_gamma, shp(qr), shp(kr), shp(vr), shp(gr), gain)
    return out.reshape(batch * seq_len, RET_V_W)


def _mix_kernel(x_ref, at_ref, ret_ref, gate_ref, wa_ref, wr_ref, wo_ref, o_ref):
    a = lax.dot_general(at_ref[0], wa_ref[...], TN_DIMS, preferred_element_type=F32)
    r = jnp.dot(ret_ref[...], wr_ref[...], preferred_element_type=F32)
    ga = gate_ref[:, :D_MODEL].astype(F32)
    gr = gate_ref[:, D_MODEL:].astype(F32)
    mixed = (ga * a + gr * r).astype(BF16)
    o_ref[...] = x_ref[...] + jnp.dot(mixed, wo_ref[...], preferred_element_type=F32)


def _mix(x2d, attn_t, ret, gates, p):
    n_tok = x2d.shape[0]
    tm = TOKEN_TILE
    tok = lambda w: pl.BlockSpec((tm, w), lambda i: (i, 0))
    return pl.pallas_call(
        _mix_kernel,
        grid=(n_tok // tm,),
        in_specs=[
            tok(D_MODEL),
            pl.BlockSpec((1, ATTN_Q_W, tm), lambda i: (i, 0, 0)),
            tok(RET_V_W), tok(2 * D_MODEL),
            _const_spec((ATTN_Q_W, D_MODEL)),
            _const_spec((RET_V_W, D_MODEL)),
            _const_spec((D_MODEL, D_MODEL)),
        ],
        out_specs=tok(D_MODEL),
        out_shape=jax.ShapeDtypeStruct((n_tok, D_MODEL), F32),
        compiler_params=pltpu.CompilerParams(
            dimension_semantics=("arbitrary",), vmem_limit_bytes=V7X_VMEM_LIMIT),
        name="mix",
    )(x2d, attn_t, ret, gates, p["w_a"], p["w_r"], p["w_o"])


def _ffn_kernel(x_ref, gffn_ref, win_ref, wout_ref, gfin_ref, o_ref):
    x = x_ref[...]
    h = (_rms(x) * gffn_ref[...]).astype(BF16)
    acc = x
    for c0, c1 in FFN_CHUNKS:
        gt = jnp.dot(h, win_ref[:, c0:c1], preferred_element_type=F32)
        up = jnp.dot(h, win_ref[:, D_FF + c0:D_FF + c1], preferred_element_type=F32)
        act = (gt / (1.0 + jnp.exp(-gt)) * up).astype(BF16)
        acc = acc + jnp.dot(act, wout_ref[c0:c1, :], preferred_element_type=F32)
    o_ref[...] = _rms(acc) * gfin_ref[...]


def _ffn(x2d, p):
    n_tok = x2d.shape[0]
    tm = TOKEN_TILE
    tok = pl.BlockSpec((tm, D_MODEL), lambda i: (i, 0))
    return pl.pallas_call(
        _ffn_kernel,
        grid=(n_tok // tm,),
        in_specs=[
            tok,
            _const_spec((1, D_MODEL)),
            _const_spec((D_MODEL, 2 * D_FF)),
            _const_spec((D_FF, D_MODEL)),
            _const_spec((1, D_MODEL)),
        ],
        out_specs=tok,
        out_shape=jax.ShapeDtypeStruct((n_tok, D_MODEL), F32),
        compiler_params=pltpu.CompilerParams(
            dimension_semantics=("arbitrary",), vmem_limit_bytes=V7X_VMEM_LIMIT),
        name="ffn",
    )(x2d, p["g_ffn"], p["w_ffn_in"], p["w_ffn_out"], p["g_fin"])


def _deinterleave(n_heads, head_dim):
    within = np.concatenate([np.arange(0, head_dim, 2), np.arange(1, head_dim, 2)])
    return (np.arange(n_heads)[:, None] * head_dim + within[None, :]).reshape(-1)


def _prepare_params(norm_mix, w_in, b_gate, q_norm, k_norm, ret_decay_fwd, ret_decay_bwd, ret_norm,
                    w_branch_attn, w_branch_ret, w_out, norm_ffn, w_ffn_in, w_ffn_out, norm_final):
    widths = [ATTN_Q_W, ATTN_KV_W, ATTN_KV_W, RET_QK_W, RET_QK_W, RET_V_W, RET_V_W, 2 * D_MODEL]
    offs = np.concatenate([[0], np.cumsum(widths)])
    seg = [w_in[:, offs[j]:offs[j + 1]] for j in range(len(widths))]
    w_qa, w_ka, w_va, w_qr, w_kr, w_vr, w_gr, w_gate = seg
    w_qa = w_qa[:, _deinterleave(ATTN_HEADS, ATTN_HEAD_DIM)]
    w_ka = w_ka[:, _deinterleave(ATTN_KV_HEADS, ATTN_HEAD_DIM)]
    w_qr = w_qr[:, _deinterleave(RET_HEADS, RET_KEY_DIM)]
    w_kr = w_kr[:, _deinterleave(RET_HEADS, RET_KEY_DIM)]
    head_perm = _deinterleave(1, ATTN_HEAD_DIM)
    q_scale = ATTN_HEAD_DIM ** -0.5 * math.log2(math.e)
    gq = (q_norm[head_perm] * q_scale).astype(F32)
    return {
        "g_mix": norm_mix.reshape(1, D_MODEL),
        "w_t": jnp.concatenate([w_qa, w_va], axis=1).T.astype(BF16),
        "w_n": jnp.concatenate([w_ka, w_qr, w_kr, w_vr, w_gr, w_gate], axis=1).astype(BF16),
        "b_gate": b_gate.reshape(1, 2 * D_MODEL),
        "gq": jnp.broadcast_to(gq[:, None], (ATTN_HEAD_DIM, TOKEN_TILE)),
        "gk": jnp.tile(k_norm[head_perm], ATTN_KV_HEADS).reshape(1, ATTN_KV_W),
        "log_gamma": jnp.stack([jax.nn.log_sigmoid(ret_decay_fwd.astype(F32)),
                                jax.nn.log_sigmoid(ret_decay_bwd.astype(F32))]),
        "ret_gain": ret_norm.reshape(1, RET_V_W),
        "w_a": w_branch_attn.astype(BF16),
        "w_r": w_branch_ret.astype(BF16),
        "w_o": w_out.astype(BF16),
        "g_ffn": norm_ffn.reshape(1, D_MODEL),
        "w_ffn_in": w_ffn_in.astype(BF16),
        "w_ffn_out": w_ffn_out.astype(BF16),
        "g_fin": norm_final.reshape(1, D_MODEL),
    }


def _rope_tables(seq_len):
    n_rows = seq_len // GRID_W
    row = jnp.repeat(jnp.arange(n_rows, dtype=F32), GRID_W)
    col = jnp.tile(jnp.arange(GRID_W, dtype=F32), n_rows)

    def cos_sin(head_dim):
        n_freq = head_dim // 4
        inv_freq = ROPE_THETA ** (-jnp.arange(n_freq, dtype=F32) / n_freq)
        ang = jnp.concatenate([row[:, None] * inv_freq, col[:, None] * inv_freq], axis=-1)
        return jnp.cos(ang), jnp.sin(ang)

    ca, sa = cos_sin(ATTN_HEAD_DIM)
    cr, sr = cos_sin(RET_KEY_DIM)
    return {
        "cat": ca.T, "sat": sa.T,
        "ck": jnp.tile(ca, (1, 2 * ATTN_KV_HEADS)),
        "sk": jnp.tile(jnp.concatenate([-sa, sa], axis=-1), (1, ATTN_KV_HEADS)),
        "cr": jnp.concatenate([cr, cr], axis=-1),
        "sr": jnp.concatenate([-sr, sr], axis=-1),
    }


def _trunk(x, params):
    batch, seq_len, _ = x.shape
    p = dict(params, **_rope_tables(seq_len))
    x2d = x.reshape(batch * seq_len, D_MODEL)
    qt, vt, k, qr, kr, vr, gr, gates = _in_proj(x2d, seq_len, p)
    attn_t = _attention(qt, k, vt, batch, seq_len)
    ret = _retention(qr, kr, vr, gr, p["log_gamma"], p["ret_gain"], batch, seq_len)
    x1 = _mix(x2d, attn_t, ret, gates, p)
    return _ffn(x1, p).reshape(batch, seq_len, D_MODEL)


def kernel(x_prompt, x_sample, norm_mix, w_in, b_gate, q_norm, k_norm, ret_decay_fwd, ret_decay_bwd,
           ret_norm, w_branch_attn, w_branch_ret, w_out, norm_ffn, w_ffn_in, w_ffn_out, norm_final):
    params = _prepare_params(norm_mix[0], w_in[0], b_gate[0], q_norm[0], k_norm[0], ret_decay_fwd[0],
                             ret_decay_bwd[0], ret_norm[0], w_branch_attn[0], w_branch_ret[0], w_out[0],
                             norm_ffn[0], w_ffn_in[0], w_ffn_out[0], norm_final)
    return _trunk(x_prompt, params), _trunk(x_sample, params)
```

```python
import functools
import math

import jax
import jax.numpy as jnp
import numpy as np
from jax import lax
from jax.experimental import pallas as pl
from jax.experimental.pallas import tpu as pltpu

F32 = jnp.float32
BF16 = jnp.bfloat16

D_MODEL = 1024
GRID_W = 64
ATTN_HEADS = 8
ATTN_KV_HEADS = 2
ATTN_GROUP = ATTN_HEADS // ATTN_KV_HEADS
ATTN_HEAD_DIM = 64
RET_HEADS = 4
RET_KEY_DIM = 128
RET_VALUE_DIM = 256
ATTN_Q_W = ATTN_HEADS * ATTN_HEAD_DIM
ATTN_KV_W = ATTN_KV_HEADS * ATTN_HEAD_DIM
RET_QK_W = RET_HEADS * RET_KEY_DIM
RET_V_W = RET_HEADS * RET_VALUE_DIM
D_FF = 2816
ROPE_THETA = 10000.0
EPS = 1e-6

TOKEN_TILE = 512
RET_CHUNK = 512
FFN_CHUNKS = ((0, 1536), (1536, 2816))
ONES_ROWS = 16
V7X_VMEM_LIMIT = 56 * 1024 * 1024

NT_DIMS = (((1,), (1,)), ((), ()))
TN_DIMS = (((0,), (0,)), ((), ()))


def _const_spec(shape):
    nd = len(shape)
    return pl.BlockSpec(shape, lambda *_: (0,) * nd)


def _rms(x):
    return x * lax.rsqrt(jnp.mean(x * x, axis=-1, keepdims=True) + EPS)


def _in_proj_kernel(x_ref, gmix_ref, wt_ref, wn_ref, bg_ref, gq_ref, gk_ref,
                    cat_ref, sat_ref, ck_ref, sk_ref, cr_ref, sr_ref,
                    qt_ref, vt_ref, k_ref, qr_ref, kr_ref, vr_ref, gr_ref, gate_ref):
    tm = x_ref.shape[0]
    h = (_rms(x_ref[...]) * gmix_ref[...]).astype(BF16)

    t = lax.dot_general(wt_ref[...], h, NT_DIMS, preferred_element_type=F32)
    cat, sat = cat_ref[...], sat_ref[...]
    gq = gq_ref[...]
    half = ATTN_HEAD_DIM // 2
    for hd in range(ATTN_HEADS):
        r0 = hd * ATTN_HEAD_DIM
        blk = t[r0:r0 + ATTN_HEAD_DIM]
        inv = lax.rsqrt(jnp.mean(blk * blk, axis=0, keepdims=True) + EPS)
        xn = blk * inv * gq
        x0, x1 = xn[:half], xn[half:]
        qt_ref[0, r0:r0 + half, :] = (x0 * cat - x1 * sat).astype(BF16)
        qt_ref[0, r0 + half:r0 + ATTN_HEAD_DIM, :] = (x0 * sat + x1 * cat).astype(BF16)
    vt_ref[0] = t[ATTN_Q_W:ATTN_Q_W + ATTN_KV_W].astype(BF16)

    o = 0
    kf = jnp.dot(h, wn_ref[:, o:o + ATTN_KV_W], preferred_element_type=F32)
    o += ATTN_KV_W
    lane = lax.broadcasted_iota(jnp.int32, (tm, ATTN_KV_W), 1)
    lo = lane < ATTN_HEAD_DIM
    k2 = kf * kf
    s_lo = jnp.sum(jnp.where(lo, k2, 0.0), axis=-1, keepdims=True)
    s_hi = jnp.sum(jnp.where(lo, 0.0, k2), axis=-1, keepdims=True)
    inv = jnp.where(lo, lax.rsqrt(s_lo * (1.0 / ATTN_HEAD_DIM) + EPS),
                    lax.rsqrt(s_hi * (1.0 / ATTN_HEAD_DIM) + EPS))
    kn = kf * inv * gk_ref[...]
    first_half = (lane % ATTN_HEAD_DIM) < half
    partner = jnp.where(first_half, pltpu.roll(kn, ATTN_KV_W - half, 1), pltpu.roll(kn, half, 1))
    k_ref[...] = (kn * ck_ref[...] + partner * sk_ref[...]).astype(BF16)

    cr, sr = cr_ref[...], sr_ref[...]
    for dst, scale in ((qr_ref, RET_KEY_DIM ** -0.5), (kr_ref, None)):
        y = jnp.dot(h, wn_ref[:, o:o + RET_QK_W], preferred_element_type=F32)
        o += RET_QK_W
        for hd in range(RET_HEADS):
            c0 = hd * RET_KEY_DIM
            xh = y[:, c0:c0 + RET_KEY_DIM]
            r = xh * cr + pltpu.roll(xh, RET_KEY_DIM // 2, 1) * sr
            if scale is not None:
                r = r * scale
            dst[:, c0:c0 + RET_KEY_DIM] = r.astype(BF16)

    for dst in (vr_ref, gr_ref):
        dst[...] = jnp.dot(h, wn_ref[:, o:o + RET_V_W], preferred_element_type=F32).astype(BF16)
        o += RET_V_W

    for c in range(2):
        z = jnp.dot(h, wn_ref[:, o:o + D_MODEL], preferred_element_type=F32)
        z = z + bg_ref[:, c * D_MODEL:(c + 1) * D_MODEL]
        gate_ref[:, c * D_MODEL:(c + 1) * D_MODEL] = (1.0 / (1.0 + jnp.exp(-z))).astype(BF16)
        o += D_MODEL


def _in_proj(x2d, seq_len, p):
    n_tok = x2d.shape[0]
    tm = TOKEN_TILE
    n_tiles = n_tok // tm
    tiles_per_seq = seq_len // tm
    wn_cols = p["w_n"].shape[1]

    def tok(w):
        return pl.BlockSpec((tm, w), lambda i: (i, 0))

    def pos_rows(w):
        return pl.BlockSpec((tm, w), lambda i: (i % tiles_per_seq, 0))

    def pos_lanes(r):
        return pl.BlockSpec((r, tm), lambda i: (0, i % tiles_per_seq))

    out_shape = (
        jax.ShapeDtypeStruct((n_tiles, ATTN_Q_W, tm), BF16),
        jax.ShapeDtypeStruct((n_tiles, ATTN_KV_W, tm), BF16),
        jax.ShapeDtypeStruct((n_tok, ATTN_KV_W), BF16),
        jax.ShapeDtypeStruct((n_tok, RET_QK_W), BF16),
        jax.ShapeDtypeStruct((n_tok, RET_QK_W), BF16),
        jax.ShapeDtypeStruct((n_tok, RET_V_W), BF16),
        jax.ShapeDtypeStruct((n_tok, RET_V_W), BF16),
        jax.ShapeDtypeStruct((n_tok, 2 * D_MODEL), BF16),
    )
    out_specs = (
        pl.BlockSpec((1, ATTN_Q_W, tm), lambda i: (i, 0, 0)),
        pl.BlockSpec((1, ATTN_KV_W, tm), lambda i: (i, 0, 0)),
        tok(ATTN_KV_W), tok(RET_QK_W), tok(RET_QK_W), tok(RET_V_W), tok(RET_V_W), tok(2 * D_MODEL),
    )
    in_specs = [
        tok(D_MODEL),
        _const_spec((1, D_MODEL)),
        _const_spec((ATTN_Q_W + ATTN_KV_W, D_MODEL)),
        _const_spec((D_MODEL, wn_cols)),
        _const_spec((1, 2 * D_MODEL)),
        _const_spec((ATTN_HEAD_DIM, tm)),
        _const_spec((1, ATTN_KV_W)),
        pos_lanes(ATTN_HEAD_DIM // 2), pos_lanes(ATTN_HEAD_DIM // 2),
        pos_rows(ATTN_KV_W), pos_rows(ATTN_KV_W),
        pos_rows(RET_KEY_DIM), pos_rows(RET_KEY_DIM),
    ]
    return pl.pallas_call(
        _in_proj_kernel,
        grid=(n_tiles,),
        in_specs=in_specs,
        out_specs=out_specs,
        out_shape=out_shape,
        compiler_params=pltpu.CompilerParams(
            dimension_semantics=("arbitrary",), vmem_limit_bytes=V7X_VMEM_LIMIT),
        name="in_proj",
    )(x2d, p["g_mix"], p["w_t"], p["w_n"], p["b_gate"], p["gq"], p["gk"],
      p["cat"], p["sat"], p["ck"], p["sk"], p["cr"], p["sr"])


def _attn_kernel(qt_ref, k_ref, vt_ref, o_ref, qpad_ref, s_ref, mprev_ref, mcur_ref, acc_ref):
    tq = qt_ref.shape[2]
    n_kv, tk = vt_ref.shape[1], vt_ref.shape[3]
    zeros = jnp.zeros((ATTN_HEAD_DIM, tq), BF16)
    ones = jnp.ones((ONES_ROWS, tk), BF16)

    def v_ext(j):
        return [jnp.concatenate([vt_ref[0, j, g * ATTN_HEAD_DIM:(g + 1) * ATTN_HEAD_DIM, :], ones], axis=0)
                for g in range(ATTN_KV_HEADS)]

    def probs(hd):
        m_cur = mcur_ref[hd]
        alpha = jnp.exp2(mprev_ref[hd] - m_cur)
        return alpha, jnp.exp2(s_ref[hd] - m_cur).astype(BF16), m_cur

    def stage_scores(hd, kb, m_cur):
        s = jnp.dot(kb, qpad_ref[hd], preferred_element_type=F32)
        s_ref[hd] = s
        mprev_ref[hd] = m_cur
        mcur_ref[hd] = jnp.maximum(m_cur, jnp.max(s, axis=0, keepdims=True))

    def accumulate(hd, alpha, pt, vext):
        pv = jnp.dot(vext[hd // ATTN_GROUP], pt, preferred_element_type=F32)
        acc_ref[hd] = alpha * acc_ref[hd] + pv

    acc_ref[...] = jnp.zeros(acc_ref.shape, F32)
    kb0 = k_ref[0, 0:tk, :]
    for hd in range(ATTN_HEADS):
        r0 = hd * ATTN_HEAD_DIM
        qh = qt_ref[0, r0:r0 + ATTN_HEAD_DIM, :]
        qpad_ref[hd] = jnp.concatenate([qh, zeros] if hd < ATTN_GROUP else [zeros, qh], axis=0)
        stage_scores(hd, kb0, jnp.full((1, tq), -jnp.inf, F32))

    def body(j, carry):
        start = pl.multiple_of((j + 1) * tk, tk)
        kb_next = k_ref[0, pl.ds(start, tk), :]
        vext = v_ext(j)
        for hd in range(ATTN_HEADS):
            alpha, pt, m_cur = probs(hd)
            stage_scores(hd, kb_next, m_cur)
            accumulate(hd, alpha, pt, vext)
        return carry

    lax.fori_loop(0, n_kv - 1, body, 0)
    vext = v_ext(n_kv - 1)
    for hd in range(ATTN_HEADS):
        alpha, pt, _ = probs(hd)
        accumulate(hd, alpha, pt, vext)
        r0 = hd * ATTN_HEAD_DIM
        acc = acc_ref[hd]
        denom = acc[ATTN_HEAD_DIM:ATTN_HEAD_DIM + 1, :]
        o_ref[0, r0:r0 + ATTN_HEAD_DIM, :] = (acc[:ATTN_HEAD_DIM] / denom).astype(BF16)


def _attention(qt, k, vt, batch, seq_len):
    tq = qt.shape[2]
    n_q = seq_len // tq
    k3 = k.reshape(batch, seq_len, ATTN_KV_W)
    vt4 = vt.reshape(batch, n_q, ATTN_KV_W, tq)
    return pl.pallas_call(
        _attn_kernel,
        grid=(batch, n_q),
        in_specs=[
            pl.BlockSpec((1, ATTN_Q_W, tq), lambda b, i: (b * n_q + i, 0, 0)),
            pl.BlockSpec((1, seq_len, ATTN_KV_W), lambda b, i: (b, 0, 0)),
            pl.BlockSpec((1, n_q, ATTN_KV_W, tq), lambda b, i: (b, 0, 0, 0)),
        ],
        out_specs=pl.BlockSpec((1, ATTN_Q_W, tq), lambda b, i: (b * n_q + i, 0, 0)),
        out_shape=jax.ShapeDtypeStruct(qt.shape, BF16),
        scratch_shapes=[
            pltpu.VMEM((ATTN_HEADS, 2 * ATTN_HEAD_DIM, tq), BF16),
            pltpu.VMEM((ATTN_HEADS, tq, tq), F32),
            pltpu.VMEM((ATTN_HEADS, 1, tq), F32),
            pltpu.VMEM((ATTN_HEADS, 1, tq), F32),
            pltpu.VMEM((ATTN_HEADS, ATTN_HEAD_DIM + ONES_ROWS, tq), F32),
        ],
        compiler_params=pltpu.CompilerParams(
            dimension_semantics=("arbitrary", "arbitrary"), vmem_limit_bytes=V7X_VMEM_LIMIT),
        name="attn",
    )(qt, k3, vt4)


def _ret_kernel(lg_ref, q_ref, k_ref, v_ref, g_ref, gain_ref, o_ref,
                sf_ref, sb_ref, sball_ref, dmask_ref, qdf_ref, qdb_ref, kdf_ref, kdb_ref):
    b = pl.program_id(0)
    sweep = pl.program_id(1)
    i = pl.program_id(2)
    n_blk = pl.num_programs(2)
    C = k_ref.shape[1]
    dk, dv = RET_KEY_DIM, RET_VALUE_DIM

    @pl.when((b == 0) & (sweep == 0) & (i == 0))
    def _decay_tables():
        diff = (lax.broadcasted_iota(jnp.int32, (C, C), 0)
                - lax.broadcasted_iota(jnp.int32, (C, C), 1)).astype(F32)
        row = lax.broadcasted_iota(jnp.int32, (C, dk), 0).astype(F32)
        for hd in range(RET_HEADS):
            lgf, lgb = lg_ref[0, hd], lg_ref[1, hd]
            dmask_ref[hd] = jnp.exp(jnp.where(diff >= 0, lgf * diff, -lgb * diff))
            qdf_ref[hd] = jnp.exp(lgf * (row + 1.0))
            qdb_ref[hd] = jnp.exp(lgb * (C - row))
            kdf_ref[hd] = jnp.exp(lgf * (C - 1.0 - row))
            kdb_ref[hd] = jnp.exp(lgb * row)

    def decayed_kv(hd, kdec_ref):
        kd = (k_ref[0, :, hd * dk:(hd + 1) * dk].astype(F32) * kdec_ref[hd]).astype(BF16)
        return lax.dot_general(kd, v_ref[0, :, hd * dv:(hd + 1) * dv], TN_DIMS, preferred_element_type=F32)

    def chunk_decay(lg):
        return jnp.exp(jnp.full((1, dv), lg * C, F32))

    @pl.when(sweep == 0)
    def _right_to_left():
        @pl.when(i == 0)
        def _():
            sb_ref[...] = jnp.zeros_like(sb_ref)

        blk = n_blk - 1 - i
        for hd in range(RET_HEADS):
            state = sb_ref[hd]
            sball_ref[blk, hd] = state.astype(BF16)
            sb_ref[hd] = chunk_decay(lg_ref[1, hd]) * state + decayed_kv(hd, kdb_ref)

    @pl.when(sweep == 1)
    def _left_to_right():
        @pl.when(i == 0)
        def _():
            sf_ref[...] = jnp.zeros_like(sf_ref)

        for hd in range(RET_HEADS):
            q = q_ref[0, :, hd * dk:(hd + 1) * dk]
            k = k_ref[0, :, hd * dk:(hd + 1) * dk]
            v = v_ref[0, :, hd * dv:(hd + 1) * dv]
            qf = q.astype(F32)
            state = sf_ref[hd]
            scores = lax.dot_general(q, k, NT_DIMS, preferred_element_type=F32) * dmask_ref[hd]
            lhs = jnp.concatenate(
                [scores.astype(BF16), (qf * qdf_ref[hd]).astype(BF16), (qf * qdb_ref[hd]).astype(BF16)], axis=1)
            rhs = jnp.concatenate([v, state.astype(BF16), sball_ref[i, hd]], axis=0)
            y = jnp.dot(lhs, rhs, preferred_element_type=F32)
            sf_ref[hd] = chunk_decay(lg_ref[0, hd]) * state + decayed_kv(hd, kdf_ref)

            mean = jnp.mean(y, axis=-1, keepdims=True)
            d = y - mean
            var = jnp.mean(d * d, axis=-1, keepdims=True)
            yn = d * lax.rsqrt(var + EPS) * gain_ref[:, hd * dv:(hd + 1) * dv]
            gl = g_ref[0, :, hd * dv:(hd + 1) * dv].astype(F32)
            o_ref[0, :, hd * dv:(hd + 1) * dv] = (gl / (1.0 + jnp.exp(-gl)) * yn).astype(BF16)


def _retention(qr, kr, vr, gr, log_gamma, gain, batch, seq_len):
    tb = RET_CHUNK
    n_blk = seq_len // tb
    shp = lambda a: a.reshape(batch, seq_len, a.shape[-1])

    def both(b, p, i):
        return (b, jnp.where(p == 0, n_blk - 1 - i, i), 0)

    def fwd_only(b, p, i):
        return (b, jnp.where(p == 0, 0, i), 0)

    out = pl.pallas_call(
        _ret_kernel,
        grid=(batch, 2, n_blk),
        in_specs=[
            pl.BlockSpec(memory_space=pltpu.SMEM),
            pl.BlockSpec((1, tb, RET_QK_W), fwd_only),
            pl.BlockSpec((1, tb, RET_QK_W), both),
            pl.BlockSpec((1, tb, RET_V_W), both),
            pl.BlockSpec((1, tb, RET_V_W), fwd_only),
            _const_spec((1, RET_V_W)),
        ],
        out_specs=pl.BlockSpec((1, tb, RET_V_W), fwd_only),
        out_shape=jax.ShapeDtypeStruct((batch, seq_len, RET_V_W), BF16),
        scratch_shapes=[
            pltpu.VMEM((RET_HEADS, RET_KEY_DIM, RET_VALUE_DIM), F32),
            pltpu.VMEM((RET_HEADS, RET_KEY_DIM, RET_VALUE_DIM), F32),
            pltpu.VMEM((n_blk, RET_HEADS, RET_KEY_DIM, RET_VALUE_DIM), BF16),
            pltpu.VMEM((RET_HEADS, tb, tb), F32),
            pltpu.VMEM((RET_HEADS, tb, RET_KEY_DIM), F32),
            pltpu.VMEM((RET_HEADS, tb, RET_KEY_DIM), F32),
            pltpu.VMEM((RET_HEADS, tb, RET_KEY_DIM), F32),
            pltpu.VMEM((RET_HEADS, tb, RET_KEY_DIM), F32),
        ],
        compiler_params=pltpu.CompilerParams(
            dimension_semantics=("arbitrary",) * 3, vmem_limit_bytes=V7X_VMEM_LIMIT),
        name="retention",
    )(log_gamma, shp(qr), shp(kr), shp(vr), shp(gr), gain)
    return out.reshape(batch * seq_len, RET_V_W)


def _mix_kernel(x_ref, at_ref, ret_ref, gate_ref, wa_ref, wr_ref, wo_ref, o_ref):
    a = lax.dot_general(at_ref[0], wa_ref[...], TN_DIMS, preferred_element_type=F32)
    r = jnp.dot(ret_ref[...], wr_ref[...], preferred_element_type=F32)
    ga = gate_ref[:, :D_MODEL].astype(F32)
    gr = gate_ref[:, D_MODEL:].astype(F32)
    mixed = (ga * a + gr * r).astype(BF16)
    o_ref[...] = x_ref[...] + jnp.dot(mixed, wo_ref[...], preferred_element_type=F32)


def _mix(x2d, attn_t, ret, gates, p):
    n_tok = x2d.shape[0]
    tm = TOKEN_TILE
    tok = lambda w: pl.BlockSpec((tm, w), lambda i: (i, 0))
    return pl.pallas_call(
        _mix_kernel,
        grid=(n_tok // tm,),
        in_specs=[
            tok(D_MODEL),
            pl.BlockSpec((1, ATTN_Q_W, tm), lambda i: (i, 0, 0)),
            tok(RET_V_W), tok(2 * D_MODEL),
            _const_spec((ATTN_Q_W, D_MODEL)),
            _const_spec((RET_V_W, D_MODEL)),
            _const_spec((D_MODEL, D_MODEL)),
        ],
        out_specs=tok(D_MODEL),
        out_shape=jax.ShapeDtypeStruct((n_tok, D_MODEL), F32),
        compiler_params=pltpu.CompilerParams(
            dimension_semantics=("arbitrary",), vmem_limit_bytes=V7X_VMEM_LIMIT),
        name="mix",
    )(x2d, attn_t, ret, gates, p["w_a"], p["w_r"], p["w_o"])


def _ffn_kernel(x_ref, gffn_ref, win_ref, wout_ref, gfin_ref, o_ref):
    x = x_ref[...]
    h = (_rms(x) * gffn_ref[...]).astype(BF16)
    acc = x
    for c0, c1 in FFN_CHUNKS:
        gt = jnp.dot(h, win_ref[:, c0:c1], preferred_element_type=F32)
        up = jnp.dot(h, win_ref[:, D_FF + c0:D_FF + c1], preferred_element_type=F32)
        act = (gt / (1.0 + jnp.exp(-gt)) * up).astype(BF16)
        acc = acc + jnp.dot(act, wout_ref[c0:c1, :], preferred_element_type=F32)
    o_ref[...] = _rms(acc) * gfin_ref[...]


def _ffn(x2d, p):
    n_tok = x2d.shape[0]
    tm = TOKEN_TILE
    tok = pl.BlockSpec((tm, D_MODEL), lambda i: (i, 0))
    return pl.pallas_call(
        _ffn_kernel,
        grid=(n_tok // tm,),
        in_specs=[
            tok,
            _const_spec((1, D_MODEL)),
            _const_spec((D_MODEL, 2 * D_FF)),
            _const_spec((D_FF, D_MODEL)),
            _const_spec((1, D_MODEL)),
        ],
        out_specs=tok,
        out_shape=jax.ShapeDtypeStruct((n_tok, D_MODEL), F32),
        compiler_params=pltpu.CompilerParams(
            dimension_semantics=("arbitrary",), vmem_limit_bytes=V7X_VMEM_LIMIT),
        name="ffn",
    )(x2d, p["g_ffn"], p["w_ffn_in"], p["w_ffn_out"], p["g_fin"])


def _deinterleave(n_heads, head_dim):
    within = np.concatenate([np.arange(0, head_dim, 2), np.arange(1, head_dim, 2)])
    return (np.arange(n_heads)[:, None] * head_dim + within[None, :]).reshape(-1)


def _prepare_params(norm_mix, w_in, b_gate, q_norm, k_norm, ret_decay_fwd, ret_decay_bwd, ret_norm,
                    w_branch_attn, w_branch_ret, w_out, norm_ffn, w_ffn_in, w_ffn_out, norm_final):
    widths = [ATTN_Q_W, ATTN_KV_W, ATTN_KV_W, RET_QK_W, RET_QK_W, RET_V_W, RET_V_W, 2 * D_MODEL]
    offs = np.concatenate([[0], np.cumsum(widths)])
    seg = [w_in[:, offs[j]:offs[j + 1]] for j in range(len(widths))]
    w_qa, w_ka, w_va, w_qr, w_kr, w_vr, w_gr, w_gate = seg
    w_qa = w_qa[:, _deinterleave(ATTN_HEADS, ATTN_HEAD_DIM)]
    w_ka = w_ka[:, _deinterleave(ATTN_KV_HEADS, ATTN_HEAD_DIM)]
    w_qr = w_qr[:, _deinterleave(RET_HEADS, RET_KEY_DIM)]
    w_kr = w_kr[:, _deinterleave(RET_HEADS, RET_KEY_DIM)]
    head_perm = _deinterleave(1, ATTN_HEAD_DIM)
    q_scale = ATTN_HEAD_DIM ** -0.5 * math.log2(math.e)
    gq = (q_norm[head_perm] * q_scale).astype(F32)
    return {
        "g_mix": norm_mix.reshape(1, D_MODEL),
        "w_t": jnp.concatenate([w_qa, w_va], axis=1).T.astype(BF16),
        "w_n": jnp.concatenate([w_ka, w_qr, w_kr, w_vr, w_gr, w_gate], axis=1).astype(BF16),
        "b_gate": b_gate.reshape(1, 2 * D_MODEL),
        "gq": jnp.broadcast_to(gq[:, None], (ATTN_HEAD_DIM, TOKEN_TILE)),
        "gk": jnp.tile(k_norm[head_perm], ATTN_KV_HEADS).reshape(1, ATTN_KV_W),
        "log_gamma": jnp.stack([jax.nn.log_sigmoid(ret_decay_fwd.astype(F32)),
                                jax.nn.log_sigmoid(ret_decay_bwd.astype(F32))]),
        "ret_gain": ret_norm.reshape(1, RET_V_W),
        "w_a": w_branch_attn.astype(BF16),
        "w_r": w_branch_ret.astype(BF16),
        "w_o": w_out.astype(BF16),
        "g_ffn": norm_ffn.reshape(1, D_MODEL),
        "w_ffn_in": w_ffn_in.astype(BF16),
        "w_ffn_out": w_ffn_out.astype(BF16),
        "g_fin": norm_final.reshape(1, D_MODEL),
    }


def _rope_tables(seq_len):
    n_rows = seq_len // GRID_W
    row = jnp.repeat(jnp.arange(n_rows, dtype=F32), GRID_W)
    col = jnp.tile(jnp.arange(GRID_W, dtype=F32), n_rows)

    def cos_sin(head_dim):
        n_freq = head_dim // 4
        inv_freq = ROPE_THETA ** (-jnp.arange(n_freq, dtype=F32) / n_freq)
        ang = jnp.concatenate([row[:, None] * inv_freq, col[:, None] * inv_freq], axis=-1)
        return jnp.cos(ang), jnp.sin(ang)

    ca, sa = cos_sin(ATTN_HEAD_DIM)
    cr, sr = cos_sin(RET_KEY_DIM)
    return {
        "cat": ca.T, "sat": sa.T,
        "ck": jnp.tile(ca, (1, 2 * ATTN_KV_HEADS)),
        "sk": jnp.tile(jnp.concatenate([-sa, sa], axis=-1), (1, ATTN_KV_HEADS)),
        "cr": jnp.concatenate([cr, cr], axis=-1),
        "sr": jnp.concatenate([-sr, sr], axis=-1),
    }


def _trunk(x, params):
    batch, seq_len, _ = x.shape
    p = dict(params, **_rope_tables(seq_len))
    x2d = x.reshape(batch * seq_len, D_MODEL)
    qt, vt, k, qr, kr, vr, gr, gates = _in_proj(x2d, seq_len, p)
    attn_t = _attention(qt, k, vt, batch, seq_len)
    ret = _retention(qr, kr, vr, gr, p["log_gamma"], p["ret_gain"], batch, seq_len)
    x1 = _mix(x2d, attn_t, ret, gates, p)
    return _ffn(x1, p).reshape(batch, seq_len, D_MODEL)


def kernel(x_prompt, x_sample, norm_mix, w_in, b_gate, q_norm, k_norm, ret_decay_fwd, ret_decay_bwd,
           ret_norm, w_branch_attn, w_branch_ret, w_out, norm_ffn, w_ffn_in, w_ffn_out, norm_final):
    params = _prepare_params(norm_mix[0], w_in[0], b_gate[0], q_norm[0], k_norm[0], ret_decay_fwd[0],
                             ret_decay_bwd[0], ret_norm[0], w_branch_attn[0], w_branch_ret[0], w_out[0],
                             norm_ffn[0], w_ffn_in[0], w_ffn_out[0], norm_final)
    return _trunk(x_prompt, params), _trunk(x_sample, params)
```

```python
import functools
import math

import jax
import jax.numpy as jnp
import numpy as np
from jax import lax
from jax.experimental import pallas as pl
from jax.experimental.pallas import tpu as pltpu

F32 = jnp.float32
BF16 = jnp.bfloat16

D_MODEL = 1024
GRID_W = 64
ATTN_HEADS = 8
ATTN_KV_HEADS = 2
ATTN_GROUP = ATTN_HEADS // ATTN_KV_HEADS
ATTN_HEAD_DIM = 64
RET_HEADS = 4
RET_KEY_DIM = 128
RET_VALUE_DIM = 256
ATTN_Q_W = ATTN_HEADS * ATTN_HEAD_DIM
ATTN_KV_W = ATTN_KV_HEADS * ATTN_HEAD_DIM
RET_QK_W = RET_HEADS * RET_KEY_DIM
RET_V_W = RET_HEADS * RET_VALUE_DIM
D_FF = 2816
ROPE_THETA = 10000.0
EPS = 1e-6

TOKEN_TILE = 512
RET_CHUNK = 512
FFN_CHUNKS = ((0, 1536), (1536, 2816))
ROW_CHAINS = 2
ONES_ROWS = 64
V7X_VMEM_LIMIT = 56 * 1024 * 1024

NT_DIMS = (((1,), (1,)), ((), ()))
TN_DIMS = (((0,), (0,)), ((), ()))


def _const_spec(shape):
    nd = len(shape)
    return pl.BlockSpec(shape, lambda *_: (0,) * nd)


def _rms(x):
    return x * lax.rsqrt(jnp.mean(x * x, axis=-1, keepdims=True) + EPS)


def _in_proj_kernel(x_ref, gmix_ref, wt_ref, wn_ref, bg_ref, gq_ref, gk_ref,
                    cat_ref, sat_ref, ck_ref, sk_ref, cr_ref, sr_ref,
                    qt_ref, vt_ref, k_ref, qr_ref, kr_ref, vr_ref, gr_ref, gate_ref):
    rows = x_ref.shape[0] // ROW_CHAINS
    half = ATTN_HEAD_DIM // 2
    lane = lax.broadcasted_iota(jnp.int32, (rows, ATTN_KV_W), 1)
    lo = lane < ATTN_HEAD_DIM
    first_half = (lane % ATTN_HEAD_DIM) < half

    for rc in range(ROW_CHAINS):
        sl = slice(rc * rows, (rc + 1) * rows)
        h = (_rms(x_ref[sl, :]) * gmix_ref[...]).astype(BF16)

        t = lax.dot_general(wt_ref[...], h, NT_DIMS, preferred_element_type=F32)
        cat, sat = cat_ref[:, sl], sat_ref[:, sl]
        gq = gq_ref[:, sl]
        for hd in range(ATTN_HEADS):
            r0 = hd * ATTN_HEAD_DIM
            blk = t[r0:r0 + ATTN_HEAD_DIM]
            inv = lax.rsqrt(jnp.mean(blk * blk, axis=0, keepdims=True) + EPS)
            xn = blk * inv * gq
            x0, x1 = xn[:half], xn[half:]
            qt_ref[0, r0:r0 + half, sl] = (x0 * cat - x1 * sat).astype(BF16)
            qt_ref[0, r0 + half:r0 + ATTN_HEAD_DIM, sl] = (x0 * sat + x1 * cat).astype(BF16)
        vt_ref[0, :, sl] = t[ATTN_Q_W:ATTN_Q_W + ATTN_KV_W].astype(BF16)

        o = 0
        kf = jnp.dot(h, wn_ref[:, o:o + ATTN_KV_W], preferred_element_type=F32)
        o += ATTN_KV_W
        k2 = kf * kf
        s_lo = jnp.sum(jnp.where(lo, k2, 0.0), axis=-1, keepdims=True)
        s_hi = jnp.sum(jnp.where(lo, 0.0, k2), axis=-1, keepdims=True)
        inv = jnp.where(lo, lax.rsqrt(s_lo * (1.0 / ATTN_HEAD_DIM) + EPS),
                        lax.rsqrt(s_hi * (1.0 / ATTN_HEAD_DIM) + EPS))
        kn = kf * inv * gk_ref[...]
        partner = jnp.where(first_half, pltpu.roll(kn, ATTN_KV_W - half, 1), pltpu.roll(kn, half, 1))
        k_ref[sl, :] = (kn * ck_ref[sl, :] + partner * sk_ref[sl, :]).astype(BF16)

        cr, sr = cr_ref[sl, :], sr_ref[sl, :]
        for dst, scale in ((qr_ref, RET_KEY_DIM ** -0.5), (kr_ref, None)):
            y = jnp.dot(h, wn_ref[:, o:o + RET_QK_W], preferred_element_type=F32)
            o += RET_QK_W
            for hd in range(RET_HEADS):
                c0 = hd * RET_KEY_DIM
                xh = y[:, c0:c0 + RET_KEY_DIM]
                r = xh * cr + pltpu.roll(xh, RET_KEY_DIM // 2, 1) * sr
                if scale is not None:
                    r = r * scale
                dst[sl, c0:c0 + RET_KEY_DIM] = r.astype(BF16)

        vr_ref[sl, :] = jnp.dot(h, wn_ref[:, o:o + RET_V_W], preferred_element_type=F32).astype(BF16)
        o += RET_V_W
        gl = jnp.dot(h, wn_ref[:, o:o + RET_V_W], preferred_element_type=F32)
        gr_ref[sl, :] = (gl / (1.0 + jnp.exp(-gl))).astype(BF16)
        o += RET_V_W

        for c in range(2):
            z = jnp.dot(h, wn_ref[:, o:o + D_MODEL], preferred_element_type=F32)
            z = z + bg_ref[:, c * D_MODEL:(c + 1) * D_MODEL]
            gate_ref[sl, c * D_MODEL:(c + 1) * D_MODEL] = (1.0 / (1.0 + jnp.exp(-z))).astype(BF16)
            o += D_MODEL


def _in_proj(x2d, seq_len, p):
    n_tok = x2d.shape[0]
    tm = TOKEN_TILE
    n_tiles = n_tok // tm
    tiles_per_seq = seq_len // tm
    wn_cols = p["w_n"].shape[1]

    def tok(w):
        return pl.BlockSpec((tm, w), lambda i: (i, 0))

    def pos_rows(w):
        return pl.BlockSpec((tm, w), lambda i: (i % tiles_per_seq, 0))

    def pos_lanes(r):
        return pl.BlockSpec((r, tm), lambda i: (0, i % tiles_per_seq))

    out_shape = (
        jax.ShapeDtypeStruct((n_tiles, ATTN_Q_W, tm), BF16),
        jax.ShapeDtypeStruct((n_tiles, ATTN_KV_W, tm), BF16),
        jax.ShapeDtypeStruct((n_tok, ATTN_KV_W), BF16),
        jax.ShapeDtypeStruct((n_tok, RET_QK_W), BF16),
        jax.ShapeDtypeStruct((n_tok, RET_QK_W), BF16),
        jax.ShapeDtypeStruct((n_tok, RET_V_W), BF16),
        jax.ShapeDtypeStruct((n_tok, RET_V_W), BF16),
        jax.ShapeDtypeStruct((n_tok, 2 * D_MODEL), BF16),
    )
    out_specs = (
        pl.BlockSpec((1, ATTN_Q_W, tm), lambda i: (i, 0, 0)),
        pl.BlockSpec((1, ATTN_KV_W, tm), lambda i: (i, 0, 0)),
        tok(ATTN_KV_W), tok(RET_QK_W), tok(RET_QK_W), tok(RET_V_W), tok(RET_V_W), tok(2 * D_MODEL),
    )
    in_specs = [
        tok(D_MODEL),
        _const_spec((1, D_MODEL)),
        _const_spec((ATTN_Q_W + ATTN_KV_W, D_MODEL)),
        _const_spec((D_MODEL, wn_cols)),
        _const_spec((1, 2 * D_MODEL)),
        _const_spec((ATTN_HEAD_DIM, tm)),
        _const_spec((1, ATTN_KV_W)),
        pos_lanes(ATTN_HEAD_DIM // 2), pos_lanes(ATTN_HEAD_DIM // 2),
        pos_rows(ATTN_KV_W), pos_rows(ATTN_KV_W),
        pos_rows(RET_KEY_DIM), pos_rows(RET_KEY_DIM),
    ]
    return pl.pallas_call(
        _in_proj_kernel,
        grid=(n_tiles,),
        in_specs=in_specs,
        out_specs=out_specs,
        out_shape=out_shape,
        compiler_params=pltpu.CompilerParams(
            dimension_semantics=("arbitrary",), vmem_limit_bytes=V7X_VMEM_LIMIT),
        name="in_proj",
    )(x2d, p["g_mix"], p["w_t"], p["w_n"], p["b_gate"], p["gq"], p["gk"],
      p["cat"], p["sat"], p["ck"], p["sk"], p["cr"], p["sr"])


def _attn_kernel(qt_ref, k_ref, vt_ref, o_ref, qpad_ref, s_ref, mprev_ref, mcur_ref, acc_ref):
    tq = qt_ref.shape[2]
    n_kv, tk = vt_ref.shape[1], vt_ref.shape[3]
    zeros = jnp.zeros((ATTN_HEAD_DIM, tq), BF16)
    ones = jnp.ones((ONES_ROWS, tk), BF16)

    def v_ext(j):
        return [jnp.concatenate([vt_ref[0, j, g * ATTN_HEAD_DIM:(g + 1) * ATTN_HEAD_DIM, :], ones], axis=0)
                for g in range(ATTN_KV_HEADS)]

    def probs(hd):
        m_cur = mcur_ref[hd]
        alpha = jnp.exp2(mprev_ref[hd] - m_cur)
        return alpha, jnp.exp2(s_ref[hd] - m_cur).astype(BF16), m_cur

    def stage_scores(hd, kb, m_cur):
        s = jnp.dot(kb, qpad_ref[hd], preferred_element_type=F32)
        s_ref[hd] = s
        mprev_ref[hd] = m_cur
        mcur_ref[hd] = jnp.maximum(m_cur, jnp.max(s, axis=0, keepdims=True))

    def accumulate(hd, alpha, pt, vext):
        pv = jnp.dot(vext[hd // ATTN_GROUP], pt, preferred_element_type=F32)
        acc_ref[hd] = alpha * acc_ref[hd] + pv

    acc_ref[...] = jnp.zeros(acc_ref.shape, F32)
    kb0 = k_ref[0, 0:tk, :]
    for hd in range(ATTN_HEADS):
        r0 = hd * ATTN_HEAD_DIM
        qh = qt_ref[0, r0:r0 + ATTN_HEAD_DIM, :]
        qpad_ref[hd] = jnp.concatenate([qh, zeros] if hd < ATTN_GROUP else [zeros, qh], axis=0)
        stage_scores(hd, kb0, jnp.full((1, tq), -jnp.inf, F32))

    def body(j, carry):
        start = pl.multiple_of((j + 1) * tk, tk)
        kb_next = k_ref[0, pl.ds(start, tk), :]
        vext = v_ext(j)
        for hd in range(ATTN_HEADS):
            alpha, pt, m_cur = probs(hd)
            stage_scores(hd, kb_next, m_cur)
            accumulate(hd, alpha, pt, vext)
        return carry

    lax.fori_loop(0, n_kv - 1, body, 0, unroll=2 if n_kv > 8 else 1)
    vext = v_ext(n_kv - 1)
    for hd in range(ATTN_HEADS):
        alpha, pt, _ = probs(hd)
        accumulate(hd, alpha, pt, vext)
        r0 = hd * ATTN_HEAD_DIM
        acc = acc_ref[hd]
        denom = acc[ATTN_HEAD_DIM:ATTN_HEAD_DIM + 1, :]
        o_ref[0, r0:r0 + ATTN_HEAD_DIM, :] = (acc[:ATTN_HEAD_DIM] / denom).astype(BF16)


def _attention(qt, k, vt, batch, seq_len):
    tq = qt.shape[2]
    n_q = seq_len // tq
    k3 = k.reshape(batch, seq_len, ATTN_KV_W)
    vt4 = vt.reshape(batch, n_q, ATTN_KV_W, tq)
    return pl.pallas_call(
        _attn_kernel,
        grid=(batch, n_q),
        in_specs=[
            pl.BlockSpec((1, ATTN_Q_W, tq), lambda b, i: (b * n_q + i, 0, 0)),
            pl.BlockSpec((1, seq_len, ATTN_KV_W), lambda b, i: (b, 0, 0)),
            pl.BlockSpec((1, n_q, ATTN_KV_W, tq), lambda b, i: (b, 0, 0, 0)),
        ],
        out_specs=pl.BlockSpec((1, ATTN_Q_W, tq), lambda b, i: (b * n_q + i, 0, 0)),
        out_shape=jax.ShapeDtypeStruct(qt.shape, BF16),
        scratch_shapes=[
            pltpu.VMEM((ATTN_HEADS, 2 * ATTN_HEAD_DIM, tq), BF16),
            pltpu.VMEM((ATTN_HEADS, tq, tq), F32),
            pltpu.VMEM((ATTN_HEADS, 1, tq), F32),
            pltpu.VMEM((ATTN_HEADS, 1, tq), F32),
            pltpu.VMEM((ATTN_HEADS, ATTN_HEAD_DIM + ONES_ROWS, tq), F32),
        ],
        compiler_params=pltpu.CompilerParams(
            dimension_semantics=("arbitrary", "arbitrary"), vmem_limit_bytes=V7X_VMEM_LIMIT),
        name="attn",
    )(qt, k3, vt4)


def _ret_kernel(lg_ref, q_ref, k_ref, v_ref, g_ref, gain_ref, o_ref,
                sf_ref, sb_ref, sball_ref, dmask_ref, qdf_ref, qdb_ref, kdf_ref, kdb_ref):
    b = pl.program_id(0)
    sweep = pl.program_id(1)
    i = pl.program_id(2)
    n_blk = pl.num_programs(2)
    C = k_ref.shape[1]
    dk, dv = RET_KEY_DIM, RET_VALUE_DIM

    @pl.when((b == 0) & (sweep == 0) & (i == 0))
    def _decay_tables():
        diff = (lax.broadcasted_iota(jnp.int32, (C, C), 0)
                - lax.broadcasted_iota(jnp.int32, (C, C), 1)).astype(F32)
        row = lax.broadcasted_iota(jnp.int32, (C, dk), 0).astype(F32)
        for hd in range(RET_HEADS):
            lgf, lgb = lg_ref[0, hd], lg_ref[1, hd]
            dmask_ref[hd] = jnp.exp(jnp.where(diff >= 0, lgf * diff, -lgb * diff))
            qdf_ref[hd] = jnp.exp(lgf * (row + 1.0))
            qdb_ref[hd] = jnp.exp(lgb * (C - row))
            kdf_ref[hd] = jnp.exp(lgf * (C - 1.0 - row))
            kdb_ref[hd] = jnp.exp(lgb * row)

    def decayed_kv(hd, kdec_ref):
        kd = (k_ref[0, :, hd * dk:(hd + 1) * dk].astype(F32) * kdec_ref[hd]).astype(BF16)
        return lax.dot_general(kd, v_ref[0, :, hd * dv:(hd + 1) * dv], TN_DIMS, preferred_element_type=F32)

    def chunk_decay(lg):
        return jnp.exp(jnp.full((1, dv), lg * C, F32))

    @pl.when(sweep == 0)
    def _right_to_left():
        @pl.when(i == 0)
        def _():
            sb_ref[...] = jnp.zeros_like(sb_ref)

        blk = n_blk - 1 - i
        for hd in range(RET_HEADS):
            state = sb_ref[hd]
            sball_ref[blk, hd] = state.astype(BF16)
            sb_ref[hd] = chunk_decay(lg_ref[1, hd]) * state + decayed_kv(hd, kdb_ref)

    @pl.when(sweep == 1)
    def _left_to_right():
        @pl.when(i == 0)
        def _():
            sf_ref[...] = jnp.zeros_like(sf_ref)

        for hd in range(RET_HEADS):
            q = q_ref[0, :, hd * dk:(hd + 1) * dk]
            k = k_ref[0, :, hd * dk:(hd + 1) * dk]
            v = v_ref[0, :, hd * dv:(hd + 1) * dv]
            qf = q.astype(F32)
            state = sf_ref[hd]
            scores = lax.dot_general(q, k, NT_DIMS, preferred_element_type=F32) * dmask_ref[hd]
            lhs = jnp.concatenate(
                [scores.astype(BF16), (qf * qdf_ref[hd]).astype(BF16), (qf * qdb_ref[hd]).astype(BF16)], axis=1)
            rhs = jnp.concatenate([v, state.astype(BF16), sball_ref[i, hd]], axis=0)
            y = jnp.dot(lhs, rhs, preferred_element_type=F32)
            sf_ref[hd] = chunk_decay(lg_ref[0, hd]) * state + decayed_kv(hd, kdf_ref)

            mean = jnp.mean(y, axis=-1, keepdims=True)
            d = y - mean
            var = jnp.mean(d * d, axis=-1, keepdims=True)
            yn = d * lax.rsqrt(var + EPS) * gain_ref[:, hd * dv:(hd + 1) * dv]
            gate = g_ref[0, :, hd * dv:(hd + 1) * dv].astype(F32)
            o_ref[0, :, hd * dv:(hd + 1) * dv] = (gate * yn).astype(BF16)


def _retention(qr, kr, vr, gr, log_gamma, gain, batch, seq_len):
    tb = RET_CHUNK
    n_blk = seq_len // tb
    shp = lambda a: a.reshape(batch, seq_len, a.shape[-1])

    def both(b, p, i):
        return (b, jnp.where(p == 0, n_blk - 1 - i, i), 0)

    def fwd_only(b, p, i):
        return (b, jnp.where(p == 0, 0, i), 0)

    out = pl.pallas_call(
        _ret_kernel,
        grid=(batch, 2, n_blk),
        in_specs=[
            pl.BlockSpec(memory_space=pltpu.SMEM),
            pl.BlockSpec((1, tb, RET_QK_W), fwd_only),
            pl.BlockSpec((1, tb, RET_QK_W), both),
            pl.BlockSpec((1, tb, RET_V_W), both),
            pl.BlockSpec((1, tb, RET_V_W), fwd_only),
            _const_spec((1, RET_V_W)),
        ],
        out_specs=pl.BlockSpec((1, tb, RET_V_W), fwd_only),
        out_shape=jax.ShapeDtypeStruct((batch, seq_len, RET_V_W), BF16),
        scratch_shapes=[
            pltpu.VMEM((RET_HEADS, RET_KEY_DIM, RET_VALUE_DIM), F32),
            pltpu.VMEM((RET_HEADS, RET_KEY_DIM, RET_VALUE_DIM), F32),
            pltpu.VMEM((n_blk, RET_HEADS, RET_KEY_DIM, RET_VALUE_DIM), BF16),
            pltpu.VMEM((RET_HEADS, tb, tb), F32),
            pltpu.VMEM((RET_HEADS, tb, RET_KEY_DIM), F32),
            pltpu.VMEM((RET_HEADS, tb, RET_KEY_DIM), F32),
            pltpu.VMEM((RET_HEADS, tb, RET_KEY_DIM), F32),
            pltpu.VMEM((RET_HEADS, tb, RET_KEY_DIM), F32),
        ],
        compiler_params=pltpu.CompilerParams(
            dimension_semantics=("arbitrary",) * 3, vmem_limit_bytes=V7X_VMEM_LIMIT),
        name="retention",
    )(log_gamma, shp(qr), shp(kr), shp(vr), shp(gr), gain)
    return out.reshape(batch * seq_len, RET_V_W)


def _mix_kernel(x_ref, at_ref, ret_ref, gate_ref, wa_ref, wr_ref, wo_ref, o_ref):
    a = lax.dot_general(at_ref[0], wa_ref[...], TN_DIMS, preferred_element_type=F32)
    r = jnp.dot(ret_ref[...], wr_ref[...], preferred_element_type=F32)
    ga = gate_ref[:, :D_MODEL].astype(F32)
    gr = gate_ref[:, D_MODEL:].astype(F32)
    mixed = (ga * a + gr * r).astype(BF16)
    o_ref[...] = x_ref[...] + jnp.dot(mixed, wo_ref[...], preferred_element_type=F32)


def _mix(x2d, attn_t, ret, gates, p):
    n_tok = x2d.shape[0]
    tm = TOKEN_TILE
    tok = lambda w: pl.BlockSpec((tm, w), lambda i: (i, 0))
    return pl.pallas_call(
        _mix_kernel,
        grid=(n_tok // tm,),
        in_specs=[
            tok(D_MODEL),
            pl.BlockSpec((1, ATTN_Q_W, tm), lambda i: (i, 0, 0)),
            tok(RET_V_W), tok(2 * D_MODEL),
            _const_spec((ATTN_Q_W, D_MODEL)),
            _const_spec((RET_V_W, D_MODEL)),
            _const_spec((D_MODEL, D_MODEL)),
        ],
        out_specs=tok(D_MODEL),
        out_shape=jax.ShapeDtypeStruct((n_tok, D_MODEL), F32),
        compiler_params=pltpu.CompilerParams(
            dimension_semantics=("arbitrary",), vmem_limit_bytes=V7X_VMEM_LIMIT),
        name="mix",
    )(x2d, attn_t, ret, gates, p["w_a"], p["w_r"], p["w_o"])


def _ffn_kernel(x_ref, gffn_ref, win_ref, wout_ref, gfin_ref, o_ref):
    rows = x_ref.shape[0] // ROW_CHAINS
    for r in range(ROW_CHAINS):
        sl = slice(r * rows, (r + 1) * rows)
        x = x_ref[sl, :]
        h = (_rms(x) * gffn_ref[...]).astype(BF16)
        acc = x
        for c0, c1 in FFN_CHUNKS:
            gt = jnp.dot(h, win_ref[:, c0:c1], preferred_element_type=F32)
            up = jnp.dot(h, win_ref[:, D_FF + c0:D_FF + c1], preferred_element_type=F32)
            act = (gt / (1.0 + jnp.exp(-gt)) * up).astype(BF16)
            acc = acc + jnp.dot(act, wout_ref[c0:c1, :], preferred_element_type=F32)
        o_ref[sl, :] = _rms(acc) * gfin_ref[...]


def _ffn(x2d, p):
    n_tok = x2d.shape[0]
    tm = TOKEN_TILE
    tok = pl.BlockSpec((tm, D_MODEL), lambda i: (i, 0))
    return pl.pallas_call(
        _ffn_kernel,
        grid=(n_tok // tm,),
        in_specs=[
            tok,
            _const_spec((1, D_MODEL)),
            _const_spec((D_MODEL, 2 * D_FF)),
            _const_spec((D_FF, D_MODEL)),
            _const_spec((1, D_MODEL)),
        ],
        out_specs=tok,
        out_shape=jax.ShapeDtypeStruct((n_tok, D_MODEL), F32),
        compiler_params=pltpu.CompilerParams(
            dimension_semantics=("arbitrary",), vmem_limit_bytes=V7X_VMEM_LIMIT),
        name="ffn",
    )(x2d, p["g_ffn"], p["w_ffn_in"], p["w_ffn_out"], p["g_fin"])


def _deinterleave(n_heads, head_dim):
    within = np.concatenate([np.arange(0, head_dim, 2), np.arange(1, head_dim, 2)])
    return (np.arange(n_heads)[:, None] * head_dim + within[None, :]).reshape(-1)


def _prepare_params(norm_mix, w_in, b_gate, q_norm, k_norm, ret_decay_fwd, ret_decay_bwd, ret_norm,
                    w_branch_attn, w_branch_ret, w_out, norm_ffn, w_ffn_in, w_ffn_out, norm_final):
    widths = [ATTN_Q_W, ATTN_KV_W, ATTN_KV_W, RET_QK_W, RET_QK_W, RET_V_W, RET_V_W, 2 * D_MODEL]
    offs = np.concatenate([[0], np.cumsum(widths)])
    seg = [w_in[:, offs[j]:offs[j + 1]] for j in range(len(widths))]
    w_qa, w_ka, w_va, w_qr, w_kr, w_vr, w_gr, w_gate = seg
    w_qa = w_qa[:, _deinterleave(ATTN_HEADS, ATTN_HEAD_DIM)]
    w_ka = w_ka[:, _deinterleave(ATTN_KV_HEADS, ATTN_HEAD_DIM)]
    w_qr = w_qr[:, _deinterleave(RET_HEADS, RET_KEY_DIM)]
    w_kr = w_kr[:, _deinterleave(RET_HEADS, RET_KEY_DIM)]
    head_perm = _deinterleave(1, ATTN_HEAD_DIM)
    q_scale = ATTN_HEAD_DIM ** -0.5 * math.log2(math.e)
    gq = (q_norm[head_perm] * q_scale).astype(F32)
    return {
        "g_mix": norm_mix.reshape(1, D_MODEL),
        "w_t": jnp.concatenate([w_qa, w_va], axis=1).T.astype(BF16),
        "w_n": jnp.concatenate([w_ka, w_qr, w_kr, w_vr, w_gr, w_gate], axis=1).astype(BF16),
        "b_gate": b_gate.reshape(1, 2 * D_MODEL),
        "gq": jnp.broadcast_to(gq[:, None], (ATTN_HEAD_DIM, TOKEN_TILE)),
        "gk": jnp.tile(k_norm[head_perm], ATTN_KV_HEADS).reshape(1, ATTN_KV_W),
        "log_gamma": jnp.stack([jax.nn.log_sigmoid(ret_decay_fwd.astype(F32)),
                                jax.nn.log_sigmoid(ret_decay_bwd.astype(F32))]),
        "ret_gain": ret_norm.reshape(1, RET_V_W),
        "w_a": w_branch_attn.astype(BF16),
        "w_r": w_branch_ret.astype(BF16),
        "w_o": w_out.astype(BF16),
        "g_ffn": norm_ffn.reshape(1, D_MODEL),
        "w_ffn_in": w_ffn_in.astype(BF16),
        "w_ffn_out": w_ffn_out.astype(BF16),
        "g_fin": norm_final.reshape(1, D_MODEL),
    }


def _rope_tables(seq_len):
    n_rows = seq_len // GRID_W
    row = jnp.arange(n_rows, dtype=F32)
    col = jnp.arange(GRID_W, dtype=F32)

    def cos_sin(head_dim):
        n_freq = head_dim // 4
        inv_freq = ROPE_THETA ** (-jnp.arange(n_freq, dtype=F32) / n_freq)
        ang_r, ang_c = row[:, None] * inv_freq, col[:, None] * inv_freq
        expand = lambda fr, fc: jnp.concatenate(
            [jnp.repeat(fr, GRID_W, axis=0), jnp.tile(fc, (n_rows, 1))], axis=-1)
        return expand(jnp.cos(ang_r), jnp.cos(ang_c)), expand(jnp.sin(ang_r), jnp.sin(ang_c))

    ca, sa = cos_sin(ATTN_HEAD_DIM)
    cr, sr = cos_sin(RET_KEY_DIM)
    return {
        "cat": ca.T, "sat": sa.T,
        "ck": jnp.tile(ca, (1, 2 * ATTN_KV_HEADS)),
        "sk": jnp.tile(jnp.concatenate([-sa, sa], axis=-1), (1, ATTN_KV_HEADS)),
        "cr": jnp.concatenate([cr, cr], axis=-1),
        "sr": jnp.concatenate([-sr, sr], axis=-1),
    }


def _trunk(x, params):
    batch, seq_len, _ = x.shape
    p = dict(params, **_rope_tables(seq_len))
    x2d = x.reshape(batch * seq_len, D_MODEL)
    qt, vt, k, qr, kr, vr, gr, gates = _in_proj(x2d, seq_len, p)
    attn_t = _attention(qt, k, vt, batch, seq_len)
    ret = _retention(qr, kr, vr, gr, p["log_gamma"], p["ret_gain"], batch, seq_len)
    x1 = _mix(x2d, attn_t, ret, gates, p)
    return _ffn(x1, p).reshape(batch, seq_len, D_MODEL)


def kernel(x_prompt, x_sample, norm_mix, w_in, b_gate, q_norm, k_norm, ret_decay_fwd, ret_decay_bwd,
           ret_norm, w_branch_attn, w_branch_ret, w_out, norm_ffn, w_ffn_in, w_ffn_out, norm_final):
    params = _prepare_params(norm_mix[0], w_in[0], b_gate[0], q_norm[0], k_norm[0], ret_decay_fwd[0],
                             ret_decay_bwd[0], ret_norm[0], w_branch_attn[0], w_branch_ret[0], w_out[0],
                             norm_ffn[0], w_ffn_in[0], w_ffn_out[0], norm_final)
    return _trunk(x_prompt, params), _trunk(x_sample, params)
```

```python
import functools
import math

import jax
import jax.numpy as jnp
import numpy as np
from jax import lax
from jax.experimental import pallas as pl
from jax.experimental.pallas import tpu as pltpu

F32 = jnp.float32
BF16 = jnp.bfloat16

D_MODEL = 1024
GRID_W = 64
ATTN_HEADS = 8
ATTN_KV_HEADS = 2
ATTN_GROUP = ATTN_HEADS // ATTN_KV_HEADS
ATTN_HEAD_DIM = 64
RET_HEADS = 4
RET_KEY_DIM = 128
RET_VALUE_DIM = 256
ATTN_Q_W = ATTN_HEADS * ATTN_HEAD_DIM
ATTN_KV_W = ATTN_KV_HEADS * ATTN_HEAD_DIM
RET_QK_W = RET_HEADS * RET_KEY_DIM
RET_V_W = RET_HEADS * RET_VALUE_DIM
D_FF = 2816
ROPE_THETA = 10000.0
EPS = 1e-6

TOKEN_TILE = 512
RET_CHUNK = 512
RET_CHUNKS_PER_STEP = 2
FFN_CHUNKS = ((0, 1536), (1536, 2816))
ROW_CHAINS = 2
ONES_ROWS = 64
V7X_VMEM_LIMIT = 56 * 1024 * 1024

NT_DIMS = (((1,), (1,)), ((), ()))
TN_DIMS = (((0,), (0,)), ((), ()))


def _const_spec(shape):
    nd = len(shape)
    return pl.BlockSpec(shape, lambda *_: (0,) * nd, pipeline_mode=pl.Buffered(1))


def _rms(x):
    return x * lax.rsqrt(jnp.mean(x * x, axis=-1, keepdims=True) + EPS)


def _sigmoid(x):
    return 1.0 / (1.0 + jnp.exp(-x))


def _in_proj_kernel(x_ref, gmix_ref, wt_ref, ws_ref, wb_ref, bg_ref, gq_ref, gk_ref,
                    tq_ref, tk_ref, tr_ref,
                    qt_ref, vt_ref, k_ref, qr_ref, kr_ref, vr_ref, gr_ref, gate_ref):
    rows = x_ref.shape[0] // ROW_CHAINS
    half = ATTN_HEAD_DIM // 2
    lane = lax.broadcasted_iota(jnp.int32, (rows, ATTN_KV_W), 1)
    lo = lane < ATTN_HEAD_DIM
    first_half = (lane % ATTN_HEAD_DIM) < half

    for rc in range(ROW_CHAINS):
        sl = slice(rc * rows, (rc + 1) * rows)
        h = (_rms(x_ref[sl, :]) * gmix_ref[...]).astype(BF16)

        t = lax.dot_general(wt_ref[...], h, NT_DIMS, preferred_element_type=F32)
        cat, sat = tq_ref[:half, sl], tq_ref[half:, sl]
        gq = gq_ref[:, sl]
        for hd in range(ATTN_HEADS):
            r0 = hd * ATTN_HEAD_DIM
            blk = t[r0:r0 + ATTN_HEAD_DIM]
            inv = lax.rsqrt(jnp.mean(blk * blk, axis=0, keepdims=True) + EPS)
            xn = blk * inv * gq
            x0, x1 = xn[:half], xn[half:]
            qt_ref[0, r0:r0 + half, sl] = (x0 * cat - x1 * sat).astype(BF16)
            qt_ref[0, r0 + half:r0 + ATTN_HEAD_DIM, sl] = (x0 * sat + x1 * cat).astype(BF16)
        vt_ref[0, :, sl] = t[ATTN_Q_W:ATTN_Q_W + ATTN_KV_W].astype(BF16)

        o = 0
        kf = jnp.dot(h, ws_ref[:, o:o + ATTN_KV_W], preferred_element_type=F32)
        o += ATTN_KV_W
        k2 = kf * kf
        s_lo = jnp.sum(jnp.where(lo, k2, 0.0), axis=-1, keepdims=True)
        s_hi = jnp.sum(jnp.where(lo, 0.0, k2), axis=-1, keepdims=True)
        inv = jnp.where(lo, lax.rsqrt(s_lo * (1.0 / ATTN_HEAD_DIM) + EPS),
                        lax.rsqrt(s_hi * (1.0 / ATTN_HEAD_DIM) + EPS))
        kn = kf * inv * gk_ref[...]
        tab = tk_ref[sl, :]
        tab_rot = pltpu.roll(tab, half, 1)
        ck = jnp.where(first_half, tab, tab_rot)
        sk = jnp.where(first_half, -tab_rot, tab)
        partner = jnp.where(first_half, pltpu.roll(kn, ATTN_KV_W - half, 1), pltpu.roll(kn, half, 1))
        k_ref[sl, :] = (kn * ck + partner * sk).astype(BF16)

        tab = tr_ref[sl, :]
        tab_rot = pltpu.roll(tab, RET_KEY_DIM // 2, 1)
        cr = jnp.where(lo, tab, tab_rot)
        sr = jnp.where(lo, -tab_rot, tab)
        for dst, scale in ((qr_ref, RET_KEY_DIM ** -0.5), (kr_ref, None)):
            y = jnp.dot(h, ws_ref[:, o:o + RET_QK_W], preferred_element_type=F32)
            o += RET_QK_W
            for hd in range(RET_HEADS):
                c0 = hd * RET_KEY_DIM
                xh = y[:, c0:c0 + RET_KEY_DIM]
                r = xh * cr + pltpu.roll(xh, RET_KEY_DIM // 2, 1) * sr
                if scale is not None:
                    r = r * scale
                dst[sl, c0:c0 + RET_KEY_DIM] = r.astype(BF16)

        gl = jnp.dot(h, wb_ref[:, RET_V_W:2 * RET_V_W], preferred_element_type=F32)
        gr_ref[sl, :] = (gl * _sigmoid(gl)).astype(BF16)
        for c in range(2):
            c0 = 2 * RET_V_W + c * D_MODEL
            z = jnp.dot(h, wb_ref[:, c0:c0 + D_MODEL], preferred_element_type=F32)
            z = z + bg_ref[:, c * D_MODEL:(c + 1) * D_MODEL]
            gate_ref[sl, c * D_MODEL:(c + 1) * D_MODEL] = _sigmoid(z).astype(BF16)
        vr_ref[sl, :] = jnp.dot(h, wb_ref[:, :RET_V_W], preferred_element_type=F32).astype(BF16)


def _in_proj(x2d, seq_len, p, tables):
    n_tok = x2d.shape[0]
    tm = TOKEN_TILE
    n_tiles = n_tok // tm
    tiles_per_seq = seq_len // tm

    def tok(w):
        return pl.BlockSpec((tm, w), lambda i: (i, 0))

    def pos_rows(w):
        return pl.BlockSpec((tm, w), lambda i: (i % tiles_per_seq, 0))

    def pos_lanes(r):
        return pl.BlockSpec((r, tm), lambda i: (0, i % tiles_per_seq))

    out_shape = (
        jax.ShapeDtypeStruct((n_tiles, ATTN_Q_W, tm), BF16),
        jax.ShapeDtypeStruct((n_tiles, ATTN_KV_W, tm), BF16),
        jax.ShapeDtypeStruct((n_tok, ATTN_KV_W), BF16),
        jax.ShapeDtypeStruct((n_tok, RET_QK_W), BF16),
        jax.ShapeDtypeStruct((n_tok, RET_QK_W), BF16),
        jax.ShapeDtypeStruct((n_tok, RET_V_W), BF16),
        jax.ShapeDtypeStruct((n_tok, RET_V_W), BF16),
        jax.ShapeDtypeStruct((n_tok, 2 * D_MODEL), BF16),
    )
    out_specs = (
        pl.BlockSpec((1, ATTN_Q_W, tm), lambda i: (i, 0, 0)),
        pl.BlockSpec((1, ATTN_KV_W, tm), lambda i: (i, 0, 0)),
        tok(ATTN_KV_W), tok(RET_QK_W), tok(RET_QK_W), tok(RET_V_W), tok(RET_V_W), tok(2 * D_MODEL),
    )
    in_specs = [
        tok(D_MODEL),
        _const_spec((1, D_MODEL)),
        _const_spec(p["w_t"].shape),
        _const_spec(p["w_s"].shape),
        _const_spec(p["w_b"].shape),
        _const_spec((1, 2 * D_MODEL)),
        _const_spec((ATTN_HEAD_DIM, tm)),
        _const_spec((1, ATTN_KV_W)),
        pos_lanes(ATTN_HEAD_DIM), pos_rows(ATTN_KV_W), pos_rows(RET_KEY_DIM),
    ]
    return pl.pallas_call(
        _in_proj_kernel,
        grid=(n_tiles,),
        in_specs=in_specs,
        out_specs=out_specs,
        out_shape=out_shape,
        compiler_params=pltpu.CompilerParams(
            dimension_semantics=("arbitrary",), vmem_limit_bytes=V7X_VMEM_LIMIT),
        name="in_proj",
    )(x2d, p["g_mix"], p["w_t"], p["w_s"], p["w_b"], p["b_gate"], p["gq"], p["gk"], *tables)


def _attn_kernel(qt_ref, qnext_ref, k_ref, vt_ref, o_ref, qpad_ref, s_ref, mprev_ref, mcur_ref, acc_ref):
    i = pl.program_id(1)
    tq = qt_ref.shape[2]
    n_kv, tk = vt_ref.shape[1], vt_ref.shape[3]
    zeros = jnp.zeros((ATTN_HEAD_DIM, tq), BF16)
    ones = jnp.ones((ONES_ROWS, tk), BF16)
    neg_inf = jnp.full((1, tq), -jnp.inf, F32)

    def v_ext(j):
        return [jnp.concatenate([vt_ref[0, j, g * ATTN_HEAD_DIM:(g + 1) * ATTN_HEAD_DIM, :], ones], axis=0)
                for g in range(ATTN_KV_HEADS)]

    def load_query(hd, ref):
        r0 = hd * ATTN_HEAD_DIM
        qh = ref[0, r0:r0 + ATTN_HEAD_DIM, :]
        qpad_ref[hd] = jnp.concatenate([qh, zeros] if hd < ATTN_GROUP else [zeros, qh], axis=0)

    def probs(hd):
        m_cur = mcur_ref[hd]
        alpha = jnp.exp2(mprev_ref[hd] - m_cur)
        return alpha, jnp.exp2(s_ref[hd] - m_cur).astype(BF16), m_cur

    def stage_scores(hd, kb, m_cur):
        s = jnp.dot(kb, qpad_ref[hd], preferred_element_type=F32)
        s_ref[hd] = s
        mprev_ref[hd] = m_cur
        mcur_ref[hd] = jnp.maximum(m_cur, jnp.max(s, axis=0, keepdims=True))

    def accumulate(hd, alpha, pt, vext):
        pv = jnp.dot(vext[hd // ATTN_GROUP], pt, preferred_element_type=F32)
        acc_ref[hd] = alpha * acc_ref[hd] + pv

    @pl.when(i == 0)
    def _first_tile_of_sequence():
        kb0 = k_ref[0, 0:tk, :]
        for hd in range(ATTN_HEADS):
            load_query(hd, qt_ref)
            stage_scores(hd, kb0, neg_inf)

    acc_ref[...] = jnp.zeros(acc_ref.shape, F32)

    def body(j, carry):
        start = pl.multiple_of((j + 1) * tk, tk)
        kb_next = k_ref[0, pl.ds(start, tk), :]
        vext = v_ext(j)
        for hd in range(ATTN_HEADS):
            alpha, pt, m_cur = probs(hd)
            stage_scores(hd, kb_next, m_cur)
            accumulate(hd, alpha, pt, vext)
        return carry

    lax.fori_loop(0, n_kv - 1, body, 0, unroll=2 if n_kv > 8 else 1)

    vext = v_ext(n_kv - 1)
    kb0 = k_ref[0, 0:tk, :]
    for hd in range(ATTN_HEADS):
        alpha, pt, _ = probs(hd)
        load_query(hd, qnext_ref)
        stage_scores(hd, kb0, neg_inf)
        accumulate(hd, alpha, pt, vext)
        r0 = hd * ATTN_HEAD_DIM
        acc = acc_ref[hd]
        denom = acc[ATTN_HEAD_DIM:ATTN_HEAD_DIM + 1, :]
        o_ref[0, r0:r0 + ATTN_HEAD_DIM, :] = (acc[:ATTN_HEAD_DIM] / denom).astype(BF16)


def _attention(qt, k, vt, batch, seq_len):
    tq = qt.shape[2]
    n_q = seq_len // tq
    k3 = k.reshape(batch, seq_len, ATTN_KV_W)
    vt4 = vt.reshape(batch, n_q, ATTN_KV_W, tq)
    return pl.pallas_call(
        _attn_kernel,
        grid=(batch, n_q),
        in_specs=[
            pl.BlockSpec((1, ATTN_Q_W, tq), lambda b, i: (b * n_q + i, 0, 0)),
            pl.BlockSpec((1, ATTN_Q_W, tq), lambda b, i: (b * n_q + jnp.minimum(i + 1, n_q - 1), 0, 0)),
            pl.BlockSpec((1, seq_len, ATTN_KV_W), lambda b, i: (b, 0, 0)),
            pl.BlockSpec((1, n_q, ATTN_KV_W, tq), lambda b, i: (b, 0, 0, 0)),
        ],
        out_specs=pl.BlockSpec((1, ATTN_Q_W, tq), lambda b, i: (b * n_q + i, 0, 0)),
        out_shape=jax.ShapeDtypeStruct(qt.shape, BF16),
        scratch_shapes=[
            pltpu.VMEM((ATTN_HEADS, 2 * ATTN_HEAD_DIM, tq), BF16),
            pltpu.VMEM((ATTN_HEADS, tq, tq), F32),
            pltpu.VMEM((ATTN_HEADS, 1, tq), F32),
            pltpu.VMEM((ATTN_HEADS, 1, tq), F32),
            pltpu.VMEM((ATTN_HEADS, ATTN_HEAD_DIM + ONES_ROWS, tq), F32),
        ],
        compiler_params=pltpu.CompilerParams(
            dimension_semantics=("arbitrary", "arbitrary"), vmem_limit_bytes=V7X_VMEM_LIMIT),
        name="attn",
    )(qt, qt, k3, vt4)


def _ret_kernel(lg_ref, q_ref, k_ref, v_ref, g_ref, gain_ref, o_ref,
                sf_ref, sb_ref, sball_ref, dmask_ref, qdf_ref, qdb_ref, kdf_ref, kdb_ref):
    b = pl.program_id(0)
    sweep = pl.program_id(1)
    i = pl.program_id(2)
    n_blk = pl.num_programs(2)
    C = RET_CHUNK
    n_sub = k_ref.shape[1] // C
    dk, dv = RET_KEY_DIM, RET_VALUE_DIM

    @pl.when((b == 0) & (sweep == 0) & (i == 0))
    def _decay_tables():
        diff = (lax.broadcasted_iota(jnp.int32, (C, C), 0)
                - lax.broadcasted_iota(jnp.int32, (C, C), 1)).astype(F32)
        row = lax.broadcasted_iota(jnp.int32, (C, dk), 0).astype(F32)
        for hd in range(RET_HEADS):
            lgf, lgb = lg_ref[0, hd], lg_ref[1, hd]
            dmask_ref[hd] = jnp.exp(jnp.where(diff >= 0, lgf * diff, -lgb * diff))
            qdf_ref[hd] = jnp.exp(lgf * (row + 1.0))
            qdb_ref[hd] = jnp.exp(lgb * (C - row))
            kdf_ref[hd] = jnp.exp(lgf * (C - 1.0 - row))
            kdb_ref[hd] = jnp.exp(lgb * row)

    def decayed_kv(hd, rows, kdec_ref):
        kd = (k_ref[0, rows, hd * dk:(hd + 1) * dk].astype(F32) * kdec_ref[hd]).astype(BF16)
        return lax.dot_general(kd, v_ref[0, rows, hd * dv:(hd + 1) * dv], TN_DIMS, preferred_element_type=F32)

    def chunk_decay(lg):
        return jnp.exp(jnp.full((1, dv), lg * C, F32))

    @pl.when(sweep == 0)
    def _right_to_left():
        @pl.when(i == 0)
        def _():
            sb_ref[...] = jnp.zeros_like(sb_ref)

        blk = n_blk - 1 - i
        for c in reversed(range(n_sub)):
            rows = slice(c * C, (c + 1) * C)
            for hd in range(RET_HEADS):
                state = sb_ref[hd]
                sball_ref[blk * n_sub + c, hd] = state.astype(BF16)
                sb_ref[hd] = chunk_decay(lg_ref[1, hd]) * state + decayed_kv(hd, rows, kdb_ref)

    @pl.when(sweep == 1)
    def _left_to_right():
        @pl.when(i == 0)
        def _():
            sf_ref[...] = jnp.zeros_like(sf_ref)

        for c in range(n_sub):
            rows = slice(c * C, (c + 1) * C)
            for hd in range(RET_HEADS):
                q = q_ref[0, rows, hd * dk:(hd + 1) * dk]
                k = k_ref[0, rows, hd * dk:(hd + 1) * dk]
                v = v_ref[0, rows, hd * dv:(hd + 1) * dv]
                qf = q.astype(F32)
                state = sf_ref[hd]
                scores = lax.dot_general(q, k, NT_DIMS, preferred_element_type=F32) * dmask_ref[hd]
                lhs = jnp.concatenate(
                    [scores.astype(BF16), (qf * qdf_ref[hd]).astype(BF16), (qf * qdb_ref[hd]).astype(BF16)],
                    axis=1)
                rhs = jnp.concatenate([v, state.astype(BF16), sball_ref[i * n_sub + c, hd]], axis=0)
                y = jnp.dot(lhs, rhs, preferred_element_type=F32)
                sf_ref[hd] = chunk_decay(lg_ref[0, hd]) * state + decayed_kv(hd, rows, kdf_ref)

                mean = jnp.mean(y, axis=-1, keepdims=True)
                d = y - mean
                var = jnp.mean(d * d, axis=-1, keepdims=True)
                yn = d * lax.rsqrt(var + EPS) * gain_ref[:, hd * dv:(hd + 1) * dv]
                gate = g_ref[0, rows, hd * dv:(hd + 1) * dv].astype(F32)
                o_ref[0, rows, hd * dv:(hd + 1) * dv] = (gate * yn).astype(BF16)


def _retention(qr, kr, vr, gr, log_gamma, gain, batch, seq_len):
    tb = RET_CHUNK * RET_CHUNKS_PER_STEP
    n_blk = seq_len // tb
    n_chunks = seq_len // RET_CHUNK
    shp = lambda a: a.reshape(batch, seq_len, a.shape[-1])

    def both(b, p, i):
        return (b, jnp.where(p == 0, n_blk - 1 - i, i), 0)

    def fwd_only(b, p, i):
        return (b, jnp.where(p == 0, 0, i), 0)

    out = pl.pallas_call(
        _ret_kernel,
        grid=(batch, 2, n_blk),
        in_specs=[
            pl.BlockSpec(memory_space=pltpu.SMEM),
            pl.BlockSpec((1, tb, RET_QK_W), fwd_only),
            pl.BlockSpec((1, tb, RET_QK_W), both),
            pl.BlockSpec((1, tb, RET_V_W), both),
            pl.BlockSpec((1, tb, RET_V_W), fwd_only),
            _const_spec((1, RET_V_W)),
        ],
        out_specs=pl.BlockSpec((1, tb, RET_V_W), fwd_only),
        out_shape=jax.ShapeDtypeStruct((batch, seq_len, RET_V_W), BF16),
        scratch_shapes=[
            pltpu.VMEM((RET_HEADS, RET_KEY_DIM, RET_VALUE_DIM), F32),
            pltpu.VMEM((RET_HEADS, RET_KEY_DIM, RET_VALUE_DIM), F32),
            pltpu.VMEM((n_chunks, RET_HEADS, RET_KEY_DIM, RET_VALUE_DIM), BF16),
            pltpu.VMEM((RET_HEADS, RET_CHUNK, RET_CHUNK), F32),
            pltpu.VMEM((RET_HEADS, RET_CHUNK, RET_KEY_DIM), F32),
            pltpu.VMEM((RET_HEADS, RET_CHUNK, RET_KEY_DIM), F32),
            pltpu.VMEM((RET_HEADS, RET_CHUNK, RET_KEY_DIM), F32),
            pltpu.VMEM((RET_HEADS, RET_CHUNK, RET_KEY_DIM), F32),
        ],
        compiler_params=pltpu.CompilerParams(
            dimension_semantics=("arbitrary",) * 3, vmem_limit_bytes=V7X_VMEM_LIMIT),
        name="retention",
    )(log_gamma, shp(qr), shp(kr), shp(vr), shp(gr), gain)
    return out.reshape(batch * seq_len, RET_V_W)


def _post_kernel(x_ref, at_ref, ret_ref, gate_ref, wa_ref, wr_ref, wo_ref,
                 gffn_ref, win_ref, wout_ref, gfin_ref, o_ref):
    rows = x_ref.shape[0] // ROW_CHAINS
    for rc in range(ROW_CHAINS):
        sl = slice(rc * rows, (rc + 1) * rows)
        a = lax.dot_general(at_ref[0, :, sl], wa_ref[...], TN_DIMS, preferred_element_type=F32)
        r = jnp.dot(ret_ref[sl, :], wr_ref[...], preferred_element_type=F32)
        ga = gate_ref[sl, :D_MODEL].astype(F32)
        gr = gate_ref[sl, D_MODEL:].astype(F32)
        mixed = (ga * a + gr * r).astype(BF16)
        x1 = x_ref[sl, :] + jnp.dot(mixed, wo_ref[...], preferred_element_type=F32)

        h = (_rms(x1) * gffn_ref[...]).astype(BF16)
        acc = x1
        for c0, c1 in FFN_CHUNKS:
            gt = jnp.dot(h, win_ref[:, c0:c1], preferred_element_type=F32)
            up = jnp.dot(h, win_ref[:, D_FF + c0:D_FF + c1], preferred_element_type=F32)
            act = (gt * _sigmoid(gt) * up).astype(BF16)
            acc = acc + jnp.dot(act, wout_ref[c0:c1, :], preferred_element_type=F32)
        o_ref[sl, :] = _rms(acc) * gfin_ref[...]


def _post(x2d, attn_t, ret, gates, p):
    n_tok = x2d.shape[0]
    tm = TOKEN_TILE
    tok = lambda w: pl.BlockSpec((tm, w), lambda i: (i, 0))
    return pl.pallas_call(
        _post_kernel,
        grid=(n_tok // tm,),
        in_specs=[
            tok(D_MODEL),
            pl.BlockSpec((1, ATTN_Q_W, tm), lambda i: (i, 0, 0)),
            tok(RET_V_W), tok(2 * D_MODEL),
            _const_spec((ATTN_Q_W, D_MODEL)),
            _const_spec((RET_V_W, D_MODEL)),
            _const_spec((D_MODEL, D_MODEL)),
            _const_spec((1, D_MODEL)),
            _const_spec((D_MODEL, 2 * D_FF)),
            _const_spec((D_FF, D_MODEL)),
            _const_spec((1, D_MODEL)),
        ],
        out_specs=tok(D_MODEL),
        out_shape=jax.ShapeDtypeStruct((n_tok, D_MODEL), F32),
        compiler_params=pltpu.CompilerParams(
            dimension_semantics=("arbitrary",), vmem_limit_bytes=V7X_VMEM_LIMIT),
        name="post",
    )(x2d, attn_t, ret, gates, p["w_a"], p["w_r"], p["w_o"],
      p["g_ffn"], p["w_ffn_in"], p["w_ffn_out"], p["g_fin"])


def _deinterleave(w, n_heads, head_dim):
    w4 = w.reshape(w.shape[0], n_heads, head_dim // 2, 2)
    return jnp.concatenate([w4[..., 0], w4[..., 1]], axis=-1).reshape(w.shape[0], n_heads * head_dim)


def _prepare_params(norm_mix, w_in, b_gate, q_norm, k_norm, ret_decay_fwd, ret_decay_bwd, ret_norm,
                    w_branch_attn, w_branch_ret, w_out, norm_ffn, w_ffn_in, w_ffn_out, norm_final):
    widths = [ATTN_Q_W, ATTN_KV_W, ATTN_KV_W, RET_QK_W, RET_QK_W]
    offs = np.concatenate([[0], np.cumsum(widths)])
    w_qa, w_ka, w_va, w_qr, w_kr = [w_in[:, offs[j]:offs[j + 1]] for j in range(len(widths))]
    w_qa = _deinterleave(w_qa, ATTN_HEADS, ATTN_HEAD_DIM)
    w_ka = _deinterleave(w_ka, ATTN_KV_HEADS, ATTN_HEAD_DIM)
    w_qr = _deinterleave(w_qr, RET_HEADS, RET_KEY_DIM)
    w_kr = _deinterleave(w_kr, RET_HEADS, RET_KEY_DIM)
    q_scale = ATTN_HEAD_DIM ** -0.5 * math.log2(math.e)
    gq = _deinterleave(q_norm.reshape(1, -1), 1, ATTN_HEAD_DIM).reshape(-1) * q_scale
    gk = _deinterleave(k_norm.reshape(1, -1), 1, ATTN_HEAD_DIM).reshape(-1)
    return {
        "g_mix": norm_mix.reshape(1, D_MODEL),
        "w_t": jnp.concatenate([w_qa, w_va], axis=1).T.astype(BF16),
        "w_s": jnp.concatenate([w_ka, w_qr, w_kr], axis=1).astype(BF16),
        "w_b": w_in[:, offs[-1]:].astype(BF16),
        "b_gate": b_gate.reshape(1, 2 * D_MODEL),
        "gq": jnp.broadcast_to(gq.astype(F32)[:, None], (ATTN_HEAD_DIM, TOKEN_TILE)),
        "gk": jnp.tile(gk, ATTN_KV_HEADS).reshape(1, ATTN_KV_W),
        "log_gamma": jnp.stack([jax.nn.log_sigmoid(ret_decay_fwd.astype(F32)),
                                jax.nn.log_sigmoid(ret_decay_bwd.astype(F32))]),
        "ret_gain": ret_norm.reshape(1, RET_V_W),
        "w_a": w_branch_attn.astype(BF16),
        "w_r": w_branch_ret.astype(BF16),
        "w_o": w_out.astype(BF16),
        "g_ffn": norm_ffn.reshape(1, D_MODEL),
        "w_ffn_in": w_ffn_in.astype(BF16),
        "w_ffn_out": w_ffn_out.astype(BF16),
        "g_fin": norm_final.reshape(1, D_MODEL),
    }


@functools.lru_cache(maxsize=None)
def _rope_tables(seq_len):
    pos = np.arange(seq_len)
    row = (pos // GRID_W).astype(np.float64)[:, None]
    col = (pos % GRID_W).astype(np.float64)[:, None]

    def cos_sin(head_dim):
        n_freq = head_dim // 4
        inv_freq = ROPE_THETA ** (-np.arange(n_freq, dtype=np.float64) / n_freq)
        ang = np.concatenate([row * inv_freq, col * inv_freq], axis=-1)
        return np.cos(ang).astype(np.float32), np.sin(ang).astype(np.float32)

    ca, sa = cos_sin(ATTN_HEAD_DIM)
    cr, sr = cos_sin(RET_KEY_DIM)
    return (np.ascontiguousarray(np.concatenate([ca, sa], axis=-1).T),
            np.concatenate([ca, sa, ca, sa], axis=-1),
            np.concatenate([cr, sr], axis=-1))


def _trunk(x, p):
    batch, seq_len, _ = x.shape
    x2d = x.reshape(batch * seq_len, D_MODEL)
    qt, vt, k, qr, kr, vr, gr, gates = _in_proj(x2d, seq_len, p, _rope_tables(seq_len))
    attn_t = _attention(qt, k, vt, batch, seq_len)
    ret = _retention(qr, kr, vr, gr, p["log_gamma"], p["ret_gain"], batch, seq_len)
    return _post(x2d, attn_t, ret, gates, p).reshape(batch, seq_len, D_MODEL)


def kernel(x_prompt, x_sample, norm_mix, w_in, b_gate, q_norm, k_norm, ret_decay_fwd, ret_decay_bwd,
           ret_norm, w_branch_attn, w_branch_ret, w_out, norm_ffn, w_ffn_in, w_ffn_out, norm_final):
    params = _prepare_params(norm_mix[0], w_in[0], b_gate[0], q_norm[0], k_norm[0], ret_decay_fwd[0],
                             ret_decay_bwd[0], ret_norm[0], w_branch_attn[0], w_branch_ret[0], w_out[0],
                             norm_ffn[0], w_ffn_in[0], w_ffn_out[0], norm_final)
    return _trunk(x_prompt, params), _trunk(x_sample, params)
```

```python
import functools
import math

import jax
import jax.numpy as jnp
import numpy as np
from jax import lax
from jax.experimental import pallas as pl
from jax.experimental.pallas import tpu as pltpu

F32 = jnp.float32
BF16 = jnp.bfloat16

D_MODEL = 1024
GRID_W = 64
ATTN_HEADS = 8
ATTN_KV_HEADS = 2
ATTN_GROUP = ATTN_HEADS // ATTN_KV_HEADS
ATTN_HEAD_DIM = 64
RET_HEADS = 4
RET_KEY_DIM = 128
RET_VALUE_DIM = 256
ATTN_Q_W = ATTN_HEADS * ATTN_HEAD_DIM
ATTN_KV_W = ATTN_KV_HEADS * ATTN_HEAD_DIM
RET_QK_W = RET_HEADS * RET_KEY_DIM
RET_V_W = RET_HEADS * RET_VALUE_DIM
D_FF = 2816
ROPE_THETA = 10000.0
EPS = 1e-6

TOKEN_TILE = 512
RET_CHUNK = 256
RET_CHUNKS_PER_STEP = 4
FFN_CHUNKS = ((0, 1536), (1536, 2816))
ROW_CHAINS = 2
ONES_ROWS = 64
V7X_VMEM_LIMIT = 56 * 1024 * 1024

NT_DIMS = (((1,), (1,)), ((), ()))
TN_DIMS = (((0,), (0,)), ((), ()))


def _const_spec(shape):
    nd = len(shape)
    return pl.BlockSpec(shape, lambda *_: (0,) * nd, pipeline_mode=pl.Buffered(1))


def _rms(x):
    return x * lax.rsqrt(jnp.mean(x * x, axis=-1, keepdims=True) + EPS)


def _sigmoid(x):
    return 1.0 / (1.0 + jnp.exp(-x))


def _in_proj_kernel(x_ref, gmix_ref, wt_ref, ws_ref, wb_ref, bg_ref, gq_ref, gk_ref, rgain_ref,
                    tq_ref, tk_ref, tr_ref,
                    qt_ref, vt_ref, k_ref, qr_ref, kr_ref, vr_ref, gr_ref, gate_ref):
    rows = x_ref.shape[0] // ROW_CHAINS
    half = ATTN_HEAD_DIM // 2
    lane = lax.broadcasted_iota(jnp.int32, (rows, ATTN_KV_W), 1)
    lo = lane < ATTN_HEAD_DIM
    first_half = (lane % ATTN_HEAD_DIM) < half

    for rc in range(ROW_CHAINS):
        sl = slice(rc * rows, (rc + 1) * rows)
        h = (_rms(x_ref[sl, :]) * gmix_ref[...]).astype(BF16)

        t = lax.dot_general(wt_ref[...], h, NT_DIMS, preferred_element_type=F32)
        cat, sat = tq_ref[:half, sl], tq_ref[half:, sl]
        gq = gq_ref[:, sl]
        for hd in range(ATTN_HEADS):
            r0 = hd * ATTN_HEAD_DIM
            blk = t[r0:r0 + ATTN_HEAD_DIM]
            inv = lax.rsqrt(jnp.mean(blk * blk, axis=0, keepdims=True) + EPS)
            xn = blk * inv * gq
            x0, x1 = xn[:half], xn[half:]
            qt_ref[0, r0:r0 + half, sl] = (x0 * cat - x1 * sat).astype(BF16)
            qt_ref[0, r0 + half:r0 + ATTN_HEAD_DIM, sl] = (x0 * sat + x1 * cat).astype(BF16)
        vt_ref[0, :, sl] = t[ATTN_Q_W:ATTN_Q_W + ATTN_KV_W].astype(BF16)

        o = 0
        kf = jnp.dot(h, ws_ref[:, o:o + ATTN_KV_W], preferred_element_type=F32)
        o += ATTN_KV_W
        k2 = kf * kf
        s_lo = jnp.sum(jnp.where(lo, k2, 0.0), axis=-1, keepdims=True)
        s_hi = jnp.sum(jnp.where(lo, 0.0, k2), axis=-1, keepdims=True)
        inv = jnp.where(lo, lax.rsqrt(s_lo * (1.0 / ATTN_HEAD_DIM) + EPS),
                        lax.rsqrt(s_hi * (1.0 / ATTN_HEAD_DIM) + EPS))
        kn = kf * inv * gk_ref[...]
        tab = tk_ref[sl, :]
        tab_rot = pltpu.roll(tab, half, 1)
        ck = jnp.where(first_half, tab, tab_rot)
        sk = jnp.where(first_half, -tab_rot, tab)
        partner = jnp.where(first_half, pltpu.roll(kn, ATTN_KV_W - half, 1), pltpu.roll(kn, half, 1))
        k_ref[sl, :] = (kn * ck + partner * sk).astype(BF16)

        tab = tr_ref[sl, :]
        tab_rot = pltpu.roll(tab, RET_KEY_DIM // 2, 1)
        cr = jnp.where(lo, tab, tab_rot)
        sr = jnp.where(lo, -tab_rot, tab)
        for dst, scale in ((qr_ref, RET_KEY_DIM ** -0.5), (kr_ref, None)):
            y = jnp.dot(h, ws_ref[:, o:o + RET_QK_W], preferred_element_type=F32)
            o += RET_QK_W
            for hd in range(RET_HEADS):
                c0 = hd * RET_KEY_DIM
                xh = y[:, c0:c0 + RET_KEY_DIM]
                r = xh * cr + pltpu.roll(xh, RET_KEY_DIM // 2, 1) * sr
                if scale is not None:
                    r = r * scale
                dst[sl, c0:c0 + RET_KEY_DIM] = r.astype(BF16)

        gl = jnp.dot(h, wb_ref[:, RET_V_W:2 * RET_V_W], preferred_element_type=F32)
        gr_ref[sl, :] = (gl * _sigmoid(gl) * rgain_ref[...]).astype(BF16)
        for c in range(2):
            c0 = 2 * RET_V_W + c * D_MODEL
            z = jnp.dot(h, wb_ref[:, c0:c0 + D_MODEL], preferred_element_type=F32)
            z = z + bg_ref[:, c * D_MODEL:(c + 1) * D_MODEL]
            gate_ref[sl, c * D_MODEL:(c + 1) * D_MODEL] = _sigmoid(z).astype(BF16)
        vr_ref[sl, :] = jnp.dot(h, wb_ref[:, :RET_V_W], preferred_element_type=F32).astype(BF16)


def _in_proj(x2d, seq_len, p, tables):
    n_tok = x2d.shape[0]
    tm = TOKEN_TILE
    n_tiles = n_tok // tm
    tiles_per_seq = seq_len // tm

    def tok(w):
        return pl.BlockSpec((tm, w), lambda i: (i, 0))

    def pos_rows(w):
        return pl.BlockSpec((tm, w), lambda i: (i % tiles_per_seq, 0))

    def pos_lanes(r):
        return pl.BlockSpec((r, tm), lambda i: (0, i % tiles_per_seq))

    out_shape = (
        jax.ShapeDtypeStruct((n_tiles, ATTN_Q_W, tm), BF16),
        jax.ShapeDtypeStruct((n_tiles, ATTN_KV_W, tm), BF16),
        jax.ShapeDtypeStruct((n_tok, ATTN_KV_W), BF16),
        jax.ShapeDtypeStruct((n_tok, RET_QK_W), BF16),
        jax.ShapeDtypeStruct((n_tok, RET_QK_W), BF16),
        jax.ShapeDtypeStruct((n_tok, RET_V_W), BF16),
        jax.ShapeDtypeStruct((n_tok, RET_V_W), BF16),
        jax.ShapeDtypeStruct((n_tok, 2 * D_MODEL), BF16),
    )
    out_specs = (
        pl.BlockSpec((1, ATTN_Q_W, tm), lambda i: (i, 0, 0)),
        pl.BlockSpec((1, ATTN_KV_W, tm), lambda i: (i, 0, 0)),
        tok(ATTN_KV_W), tok(RET_QK_W), tok(RET_QK_W), tok(RET_V_W), tok(RET_V_W), tok(2 * D_MODEL),
    )
    in_specs = [
        tok(D_MODEL),
        _const_spec((1, D_MODEL)),
        _const_spec(p["w_t"].shape),
        _const_spec(p["w_s"].shape),
        _const_spec(p["w_b"].shape),
        _const_spec((1, 2 * D_MODEL)),
        _const_spec((ATTN_HEAD_DIM, tm)),
        _const_spec((1, ATTN_KV_W)),
        _const_spec((1, RET_V_W)),
        pos_lanes(ATTN_HEAD_DIM), pos_rows(ATTN_KV_W), pos_rows(RET_KEY_DIM),
    ]
    return pl.pallas_call(
        _in_proj_kernel,
        grid=(n_tiles,),
        in_specs=in_specs,
        out_specs=out_specs,
        out_shape=out_shape,
        compiler_params=pltpu.CompilerParams(
            dimension_semantics=("arbitrary",), vmem_limit_bytes=V7X_VMEM_LIMIT),
        name="in_proj",
    )(x2d, p["g_mix"], p["w_t"], p["w_s"], p["w_b"], p["b_gate"], p["gq"], p["gk"], p["ret_gain"], *tables)


def _attn_kernel(qt_ref, qnext_ref, k_ref, vt_ref, o_ref, qpad_ref, s_ref, mprev_ref, mcur_ref, acc_ref):
    i = pl.program_id(1)
    tq = qt_ref.shape[2]
    n_kv, tk = vt_ref.shape[1], vt_ref.shape[3]
    zeros = jnp.zeros((ATTN_HEAD_DIM, tq), BF16)
    ones = jnp.ones((ONES_ROWS, tk), BF16)
    neg_inf = jnp.full((1, tq), -jnp.inf, F32)

    def v_ext(j):
        return [jnp.concatenate([vt_ref[0, j, g * ATTN_HEAD_DIM:(g + 1) * ATTN_HEAD_DIM, :], ones], axis=0)
                for g in range(ATTN_KV_HEADS)]

    def load_query(hd, ref):
        r0 = hd * ATTN_HEAD_DIM
        qh = ref[0, r0:r0 + ATTN_HEAD_DIM, :]
        qpad_ref[hd] = jnp.concatenate([qh, zeros] if hd < ATTN_GROUP else [zeros, qh], axis=0)

    def probs(hd):
        m_cur = mcur_ref[hd]
        alpha = jnp.exp2(mprev_ref[hd] - m_cur)
        return alpha, jnp.exp2(s_ref[hd] - m_cur).astype(BF16), m_cur

    def stage_scores(hd, kb, m_cur):
        s = jnp.dot(kb, qpad_ref[hd], preferred_element_type=F32)
        s_ref[hd] = s
        mprev_ref[hd] = m_cur
        mcur_ref[hd] = jnp.maximum(m_cur, jnp.max(s, axis=0, keepdims=True))

    def accumulate(hd, alpha, pt, vext):
        pv = jnp.dot(vext[hd // ATTN_GROUP], pt, preferred_element_type=F32)
        acc_ref[hd] = alpha * acc_ref[hd] + pv

    @pl.when(i == 0)
    def _first_tile_of_sequence():
        kb0 = k_ref[0, 0:tk, :]
        for hd in range(ATTN_HEADS):
            load_query(hd, qt_ref)
            stage_scores(hd, kb0, neg_inf)

    acc_ref[...] = jnp.zeros(acc_ref.shape, F32)

    def body(j, carry):
        start = pl.multiple_of((j + 1) * tk, tk)
        kb_next = k_ref[0, pl.ds(start, tk), :]
        vext = v_ext(j)
        for hd in range(ATTN_HEADS):
            alpha, pt, m_cur = probs(hd)
            stage_scores(hd, kb_next, m_cur)
            accumulate(hd, alpha, pt, vext)
        return carry

    lax.fori_loop(0, n_kv - 1, body, 0, unroll=2 if n_kv > 8 else 1)

    vext = v_ext(n_kv - 1)
    kb0 = k_ref[0, 0:tk, :]
    for hd in range(ATTN_HEADS):
        alpha, pt, _ = probs(hd)
        load_query(hd, qnext_ref)
        stage_scores(hd, kb0, neg_inf)
        accumulate(hd, alpha, pt, vext)
        r0 = hd * ATTN_HEAD_DIM
        acc = acc_ref[hd]
        denom = acc[ATTN_HEAD_DIM:ATTN_HEAD_DIM + 1, :]
        o_ref[0, r0:r0 + ATTN_HEAD_DIM, :] = (acc[:ATTN_HEAD_DIM] / denom).astype(BF16)


def _attention(qt, k, vt, batch, seq_len):
    tq = qt.shape[2]
    n_q = seq_len // tq
    k3 = k.reshape(batch, seq_len, ATTN_KV_W)
    vt4 = vt.reshape(batch, n_q, ATTN_KV_W, tq)
    return pl.pallas_call(
        _attn_kernel,
        grid=(batch, n_q),
        in_specs=[
            pl.BlockSpec((1, ATTN_Q_W, tq), lambda b, i: (b * n_q + i, 0, 0)),
            pl.BlockSpec((1, ATTN_Q_W, tq), lambda b, i: (b * n_q + jnp.minimum(i + 1, n_q - 1), 0, 0)),
            pl.BlockSpec((1, seq_len, ATTN_KV_W), lambda b, i: (b, 0, 0)),
            pl.BlockSpec((1, n_q, ATTN_KV_W, tq), lambda b, i: (b, 0, 0, 0)),
        ],
        out_specs=pl.BlockSpec((1, ATTN_Q_W, tq), lambda b, i: (b * n_q + i, 0, 0)),
        out_shape=jax.ShapeDtypeStruct(qt.shape, BF16),
        scratch_shapes=[
            pltpu.VMEM((ATTN_HEADS, 2 * ATTN_HEAD_DIM, tq), BF16),
            pltpu.VMEM((ATTN_HEADS, tq, tq), F32),
            pltpu.VMEM((ATTN_HEADS, 1, tq), F32),
            pltpu.VMEM((ATTN_HEADS, 1, tq), F32),
            pltpu.VMEM((ATTN_HEADS, ATTN_HEAD_DIM + ONES_ROWS, tq), F32),
        ],
        compiler_params=pltpu.CompilerParams(
            dimension_semantics=("arbitrary", "arbitrary"), vmem_limit_bytes=V7X_VMEM_LIMIT),
        name="attn",
    )(qt, qt, k3, vt4)


def _ret_kernel(lg_ref, q_ref, k_ref, v_ref, g_ref, o_ref,
                sf_ref, sb_ref, sball_ref, dmask_ref, qdf_ref, qdb_ref, kdf_ref, kdb_ref):
    b = pl.program_id(0)
    sweep = pl.program_id(1)
    i = pl.program_id(2)
    n_blk = pl.num_programs(2)
    C = RET_CHUNK
    n_sub = k_ref.shape[1] // C
    dk, dv = RET_KEY_DIM, RET_VALUE_DIM

    @pl.when((b == 0) & (sweep == 0) & (i == 0))
    def _decay_tables():
        diff = (lax.broadcasted_iota(jnp.int32, (C, C), 0)
                - lax.broadcasted_iota(jnp.int32, (C, C), 1)).astype(F32)
        row = lax.broadcasted_iota(jnp.int32, (C, dk), 0).astype(F32)
        for hd in range(RET_HEADS):
            lgf, lgb = lg_ref[0, hd], lg_ref[1, hd]
            dmask_ref[hd] = jnp.exp(jnp.where(diff >= 0, lgf * diff, -lgb * diff))
            qdf_ref[hd] = jnp.exp(lgf * (row + 1.0)).astype(BF16)
            qdb_ref[hd] = jnp.exp(lgb * (C - row)).astype(BF16)
            kdf_ref[hd] = jnp.exp(lgf * (C - 1.0 - row)).astype(BF16)
            kdb_ref[hd] = jnp.exp(lgb * row).astype(BF16)

    def decayed_kv(hd, rows, kdec_ref):
        kd = k_ref[0, rows, hd * dk:(hd + 1) * dk] * kdec_ref[hd]
        return lax.dot_general(kd, v_ref[0, rows, hd * dv:(hd + 1) * dv], TN_DIMS, preferred_element_type=F32)

    def chunk_decay(lg):
        return jnp.exp(jnp.full((1, dv), lg * C, F32))

    @pl.when(sweep == 0)
    def _right_to_left():
        @pl.when(i == 0)
        def _():
            sb_ref[...] = jnp.zeros_like(sb_ref)

        blk = n_blk - 1 - i
        for c in reversed(range(n_sub)):
            rows = slice(c * C, (c + 1) * C)
            for hd in range(RET_HEADS):
                state = sb_ref[hd]
                sball_ref[blk * n_sub + c, hd] = state.astype(BF16)
                sb_ref[hd] = chunk_decay(lg_ref[1, hd]) * state + decayed_kv(hd, rows, kdb_ref)

    @pl.when(sweep == 1)
    def _left_to_right():
        @pl.when(i == 0)
        def _():
            sf_ref[...] = jnp.zeros_like(sf_ref)

        for c in range(n_sub):
            rows = slice(c * C, (c + 1) * C)
            for hd in range(RET_HEADS):
                q = q_ref[0, rows, hd * dk:(hd + 1) * dk]
                k = k_ref[0, rows, hd * dk:(hd + 1) * dk]
                v = v_ref[0, rows, hd * dv:(hd + 1) * dv]
                state = sf_ref[hd]
                scores = lax.dot_general(q, k, NT_DIMS, preferred_element_type=F32) * dmask_ref[hd]
                lhs = jnp.concatenate(
                    [scores.astype(BF16), q * qdf_ref[hd], q * qdb_ref[hd]], axis=1)
                rhs = jnp.concatenate([v, state.astype(BF16), sball_ref[i * n_sub + c, hd]], axis=0)
                y = jnp.dot(lhs, rhs, preferred_element_type=F32)
                sf_ref[hd] = chunk_decay(lg_ref[0, hd]) * state + decayed_kv(hd, rows, kdf_ref)

                mean = jnp.mean(y, axis=-1, keepdims=True)
                d = y - mean
                var = jnp.mean(d * d, axis=-1, keepdims=True)
                yn = d * lax.rsqrt(var + EPS)
                gate = g_ref[0, rows, hd * dv:(hd + 1) * dv].astype(F32)
                o_ref[0, rows, hd * dv:(hd + 1) * dv] = (gate * yn).astype(BF16)


def _retention(qr, kr, vr, gr, log_gamma, batch, seq_len):
    tb = RET_CHUNK * RET_CHUNKS_PER_STEP
    n_blk = seq_len // tb
    n_chunks = seq_len // RET_CHUNK
    shp = lambda a: a.reshape(batch, seq_len, a.shape[-1])

    def both(b, p, i):
        return (b, jnp.where(p == 0, n_blk - 1 - i, i), 0)

    def fwd_only(b, p, i):
        return (b, jnp.where(p == 0, 0, i), 0)

    out = pl.pallas_call(
        _ret_kernel,
        grid=(batch, 2, n_blk),
        in_specs=[
            pl.BlockSpec(memory_space=pltpu.SMEM),
            pl.BlockSpec((1, tb, RET_QK_W), fwd_only),
            pl.BlockSpec((1, tb, RET_QK_W), both),
            pl.BlockSpec((1, tb, RET_V_W), both),
            pl.BlockSpec((1, tb, RET_V_W), fwd_only),
        ],
        out_specs=pl.BlockSpec((1, tb, RET_V_W), fwd_only),
        out_shape=jax.ShapeDtypeStruct((batch, seq_len, RET_V_W), BF16),
        scratch_shapes=[
            pltpu.VMEM((RET_HEADS, RET_KEY_DIM, RET_VALUE_DIM), F32),
            pltpu.VMEM((RET_HEADS, RET_KEY_DIM, RET_VALUE_DIM), F32),
            pltpu.VMEM((n_chunks, RET_HEADS, RET_KEY_DIM, RET_VALUE_DIM), BF16),
            pltpu.VMEM((RET_HEADS, RET_CHUNK, RET_CHUNK), F32),
            pltpu.VMEM((RET_HEADS, RET_CHUNK, RET_KEY_DIM), BF16),
            pltpu.VMEM((RET_HEADS, RET_CHUNK, RET_KEY_DIM), BF16),
            pltpu.VMEM((RET_HEADS, RET_CHUNK, RET_KEY_DIM), BF16),
            pltpu.VMEM((RET_HEADS, RET_CHUNK, RET_KEY_DIM), BF16),
        ],
        compiler_params=pltpu.CompilerParams(
            dimension_semantics=("arbitrary",) * 3, vmem_limit_bytes=V7X_VMEM_LIMIT),
        name="retention",
    )(log_gamma, shp(qr), shp(kr), shp(vr), shp(gr))
    return out.reshape(batch * seq_len, RET_V_W)


def _post_kernel(x_ref, at_ref, ret_ref, gate_ref, wa_ref, wr_ref, wo_ref,
                 gffn_ref, win_ref, wout_ref, gfin_ref, o_ref):
    rows = x_ref.shape[0] // ROW_CHAINS
    chains = [slice(rc * rows, (rc + 1) * rows) for rc in range(ROW_CHAINS)]
    x1s = []
    for sl in chains:
        r = jnp.dot(ret_ref[sl, :], wr_ref[...], preferred_element_type=F32)
        a = lax.dot_general(at_ref[0, :, sl], wa_ref[...], TN_DIMS, preferred_element_type=F32)
        ga = gate_ref[sl, :D_MODEL].astype(F32)
        gr = gate_ref[sl, D_MODEL:].astype(F32)
        mixed = (ga * a + gr * r).astype(BF16)
        x1s.append(x_ref[sl, :] + jnp.dot(mixed, wo_ref[...], preferred_element_type=F32))
    for sl, x1 in zip(chains, x1s):
        h = (_rms(x1) * gffn_ref[...]).astype(BF16)
        acc = x1
        for c0, c1 in FFN_CHUNKS:
            gt = jnp.dot(h, win_ref[:, c0:c1], preferred_element_type=F32)
            up = jnp.dot(h, win_ref[:, D_FF + c0:D_FF + c1], preferred_element_type=F32)
            act = (gt * _sigmoid(gt) * up).astype(BF16)
            acc = acc + jnp.dot(act, wout_ref[c0:c1, :], preferred_element_type=F32)
        o_ref[sl, :] = _rms(acc) * gfin_ref[...]


def _post(x2d, attn_t, ret, gates, p):
    n_tok = x2d.shape[0]
    tm = TOKEN_TILE
    tok = lambda w: pl.BlockSpec((tm, w), lambda i: (i, 0))
    return pl.pallas_call(
        _post_kernel,
        grid=(n_tok // tm,),
        in_specs=[
            tok(D_MODEL),
            pl.BlockSpec((1, ATTN_Q_W, tm), lambda i: (i, 0, 0)),
            tok(RET_V_W), tok(2 * D_MODEL),
            _const_spec((ATTN_Q_W, D_MODEL)),
            _const_spec((RET_V_W, D_MODEL)),
            _const_spec((D_MODEL, D_MODEL)),
            _const_spec((1, D_MODEL)),
            _const_spec((D_MODEL, 2 * D_FF)),
            _const_spec((D_FF, D_MODEL)),
            _const_spec((1, D_MODEL)),
        ],
        out_specs=tok(D_MODEL),
        out_shape=jax.ShapeDtypeStruct((n_tok, D_MODEL), F32),
        compiler_params=pltpu.CompilerParams(
            dimension_semantics=("arbitrary",), vmem_limit_bytes=V7X_VMEM_LIMIT),
        name="post",
    )(x2d, attn_t, ret, gates, p["w_a"], p["w_r"], p["w_o"],
      p["g_ffn"], p["w_ffn_in"], p["w_ffn_out"], p["g_fin"])


def _deinterleave(w, n_heads, head_dim):
    w4 = w.reshape(w.shape[0], n_heads, head_dim // 2, 2)
    return jnp.concatenate([w4[..., 0], w4[..., 1]], axis=-1).reshape(w.shape[0], n_heads * head_dim)


def _prepare_params(norm_mix, w_in, b_gate, q_norm, k_norm, ret_decay_fwd, ret_decay_bwd, ret_norm,
                    w_branch_attn, w_branch_ret, w_out, norm_ffn, w_ffn_in, w_ffn_out, norm_final):
    widths = [ATTN_Q_W, ATTN_KV_W, ATTN_KV_W, RET_QK_W, RET_QK_W]
    offs = np.concatenate([[0], np.cumsum(widths)])
    w_qa, w_ka, w_va, w_qr, w_kr = [w_in[:, offs[j]:offs[j + 1]] for j in range(len(widths))]
    w_qa = _deinterleave(w_qa, ATTN_HEADS, ATTN_HEAD_DIM)
    w_ka = _deinterleave(w_ka, ATTN_KV_HEADS, ATTN_HEAD_DIM)
    w_qr = _deinterleave(w_qr, RET_HEADS, RET_KEY_DIM)
    w_kr = _deinterleave(w_kr, RET_HEADS, RET_KEY_DIM)
    q_scale = ATTN_HEAD_DIM ** -0.5 * math.log2(math.e)
    gq = _deinterleave(q_norm.reshape(1, -1), 1, ATTN_HEAD_DIM).reshape(-1) * q_scale
    gk = _deinterleave(k_norm.reshape(1, -1), 1, ATTN_HEAD_DIM).reshape(-1)
    return {
        "g_mix": norm_mix.reshape(1, D_MODEL),
        "w_t": jnp.concatenate([w_qa, w_va], axis=1).T.astype(BF16),
        "w_s": jnp.concatenate([w_ka, w_qr, w_kr], axis=1).astype(BF16),
        "w_b": w_in[:, offs[-1]:].astype(BF16),
        "b_gate": b_gate.reshape(1, 2 * D_MODEL),
        "gq": jnp.broadcast_to(gq.astype(F32)[:, None], (ATTN_HEAD_DIM, TOKEN_TILE)),
        "gk": jnp.tile(gk, ATTN_KV_HEADS).reshape(1, ATTN_KV_W),
        "log_gamma": jnp.stack([jax.nn.log_sigmoid(ret_decay_fwd.astype(F32)),
                                jax.nn.log_sigmoid(ret_decay_bwd.astype(F32))]),
        "ret_gain": ret_norm.reshape(1, RET_V_W),
        "w_a": w_branch_attn.astype(BF16),
        "w_r": w_branch_ret.astype(BF16),
        "w_o": w_out.astype(BF16),
        "g_ffn": norm_ffn.reshape(1, D_MODEL),
        "w_ffn_in": w_ffn_in.astype(BF16),
        "w_ffn_out": w_ffn_out.astype(BF16),
        "g_fin": norm_final.reshape(1, D_MODEL),
    }


@functools.lru_cache(maxsize=None)
def _rope_tables(seq_len):
    pos = np.arange(seq_len)
    row = (pos // GRID_W).astype(np.float64)[:, None]
    col = (pos % GRID_W).astype(np.float64)[:, None]

    def cos_sin(head_dim):
        n_freq = head_dim // 4
        inv_freq = ROPE_THETA ** (-np.arange(n_freq, dtype=np.float64) / n_freq)
        ang = np.concatenate([row * inv_freq, col * inv_freq], axis=-1)
        return np.cos(ang).astype(np.float32), np.sin(ang).astype(np.float32)

    ca, sa = cos_sin(ATTN_HEAD_DIM)
    cr, sr = cos_sin(RET_KEY_DIM)
    return (np.ascontiguousarray(np.concatenate([ca, sa], axis=-1).T),
            np.concatenate([ca, sa, ca, sa], axis=-1),
            np.concatenate([cr, sr], axis=-1))


def _trunk(x, p):
    batch, seq_len, _ = x.shape
    x2d = x.reshape(batch * seq_len, D_MODEL)
    qt, vt, k, qr, kr, vr, gr, gates = _in_proj(x2d, seq_len, p, _rope_tables(seq_len))
    attn_t = _attention(qt, k, vt, batch, seq_len)
    ret = _retention(qr, kr, vr, gr, p["log_gamma"], batch, seq_len)
    return _post(x2d, attn_t, ret, gates, p).reshape(batch, seq_len, D_MODEL)


def kernel(x_prompt, x_sample, norm_mix, w_in, b_gate, q_norm, k_norm, ret_decay_fwd, ret_decay_bwd,
           ret_norm, w_branch_attn, w_branch_ret, w_out, norm_ffn, w_ffn_in, w_ffn_out, norm_final):
    params = _prepare_params(norm_mix[0], w_in[0], b_gate[0], q_norm[0], k_norm[0], ret_decay_fwd[0],
                             ret_decay_bwd[0], ret_norm[0], w_branch_attn[0], w_branch_ret[0], w_out[0],
                             norm_ffn[0], w_ffn_in[0], w_ffn_out[0], norm_final)
    return _trunk(x_prompt, params), _trunk(x_sample, params)
```

```python
import functools
import math

import jax
import jax.numpy as jnp
import numpy as np
from jax import lax
from jax.experimental import pallas as pl
from jax.experimental.pallas import tpu as pltpu

F32 = jnp.float32
BF16 = jnp.bfloat16

D_MODEL = 1024
GRID_W = 64
ATTN_HEADS = 8
ATTN_KV_HEADS = 2
ATTN_GROUP = ATTN_HEADS // ATTN_KV_HEADS
ATTN_HEAD_DIM = 64
RET_HEADS = 4
RET_KEY_DIM = 128
RET_VALUE_DIM = 256
ATTN_Q_W = ATTN_HEADS * ATTN_HEAD_DIM
ATTN_KV_W = ATTN_KV_HEADS * ATTN_HEAD_DIM
RET_QK_W = RET_HEADS * RET_KEY_DIM
RET_V_W = RET_HEADS * RET_VALUE_DIM
D_FF = 2816
ROPE_THETA = 10000.0
EPS = 1e-6

TOKEN_TILE = 512
ATTN_Q_TILES = 2
RET_CHUNK = 256
RET_CHUNKS_PER_STEP = 4
FFN_CHUNKS = ((0, 1536), (1536, 2816))
ROW_CHAINS = 2
ONES_ROWS = 64
V7X_VMEM_LIMIT = 56 * 1024 * 1024

NT_DIMS = (((1,), (1,)), ((), ()))
TN_DIMS = (((0,), (0,)), ((), ()))


def _const_spec(shape):
    nd = len(shape)
    return pl.BlockSpec(shape, lambda *_: (0,) * nd, pipeline_mode=pl.Buffered(1))


def _rms(x):
    return x * lax.rsqrt(jnp.mean(x * x, axis=-1, keepdims=True) + EPS)


def _sigmoid(x):
    return 1.0 / (1.0 + jnp.exp(-x))


def _in_proj_kernel(x_ref, gmix_ref, wt_ref, ws_ref, wb_ref, bg_ref, gq_ref, gk_ref, rgain_ref,
                    tq_ref, tk_ref, tr_ref,
                    qt_ref, vt_ref, k_ref, qr_ref, kr_ref, vr_ref, gr_ref, gate_ref):
    rows = x_ref.shape[0] // ROW_CHAINS
    half = ATTN_HEAD_DIM // 2
    lane = lax.broadcasted_iota(jnp.int32, (rows, ATTN_KV_W), 1)
    lo = lane < ATTN_HEAD_DIM
    first_half = (lane % ATTN_HEAD_DIM) < half

    for rc in range(ROW_CHAINS):
        sl = slice(rc * rows, (rc + 1) * rows)
        h = (_rms(x_ref[sl, :]) * gmix_ref[...]).astype(BF16)

        t = lax.dot_general(wt_ref[...], h, NT_DIMS, preferred_element_type=F32)
        cat, sat = tq_ref[:half, sl], tq_ref[half:, sl]
        gq = gq_ref[:, sl]
        for hd in range(ATTN_HEADS):
            r0 = hd * ATTN_HEAD_DIM
            blk = t[r0:r0 + ATTN_HEAD_DIM]
            inv = lax.rsqrt(jnp.mean(blk * blk, axis=0, keepdims=True) + EPS)
            xn = blk * inv * gq
            x0, x1 = xn[:half], xn[half:]
            qt_ref[0, r0:r0 + half, sl] = (x0 * cat - x1 * sat).astype(BF16)
            qt_ref[0, r0 + half:r0 + ATTN_HEAD_DIM, sl] = (x0 * sat + x1 * cat).astype(BF16)
        vt_ref[0, :, sl] = t[ATTN_Q_W:ATTN_Q_W + ATTN_KV_W].astype(BF16)

        o = 0
        kf = jnp.dot(h, ws_ref[:, o:o + ATTN_KV_W], preferred_element_type=F32)
        o += ATTN_KV_W
        k2 = kf * kf
        s_lo = jnp.sum(jnp.where(lo, k2, 0.0), axis=-1, keepdims=True)
        s_hi = jnp.sum(jnp.where(lo, 0.0, k2), axis=-1, keepdims=True)
        inv = jnp.where(lo, lax.rsqrt(s_lo * (1.0 / ATTN_HEAD_DIM) + EPS),
                        lax.rsqrt(s_hi * (1.0 / ATTN_HEAD_DIM) + EPS))
        kn = kf * inv * gk_ref[...]
        tab = tk_ref[sl, :]
        tab_rot = pltpu.roll(tab, half, 1)
        ck = jnp.where(first_half, tab, tab_rot)
        sk = jnp.where(first_half, -tab_rot, tab)
        partner = jnp.where(first_half, pltpu.roll(kn, ATTN_KV_W - half, 1), pltpu.roll(kn, half, 1))
        k_ref[sl, :] = (kn * ck + partner * sk).astype(BF16)

        tab = tr_ref[sl, :]
        tab_rot = pltpu.roll(tab, RET_KEY_DIM // 2, 1)
        cr = jnp.where(lo, tab, tab_rot)
        sr = jnp.where(lo, -tab_rot, tab)
        for dst, scale in ((qr_ref, RET_KEY_DIM ** -0.5), (kr_ref, None)):
            y = jnp.dot(h, ws_ref[:, o:o + RET_QK_W], preferred_element_type=F32)
            o += RET_QK_W
            for hd in range(RET_HEADS):
                c0 = hd * RET_KEY_DIM
                xh = y[:, c0:c0 + RET_KEY_DIM]
                r = xh * cr + pltpu.roll(xh, RET_KEY_DIM // 2, 1) * sr
                if scale is not None:
                    r = r * scale
                dst[sl, c0:c0 + RET_KEY_DIM] = r.astype(BF16)

        gl = jnp.dot(h, wb_ref[:, RET_V_W:2 * RET_V_W], preferred_element_type=F32)
        gr_ref[sl, :] = (gl * _sigmoid(gl) * rgain_ref[...]).astype(BF16)
        for c in range(2):
            c0 = 2 * RET_V_W + c * D_MODEL
            z = jnp.dot(h, wb_ref[:, c0:c0 + D_MODEL], preferred_element_type=F32)
            z = z + bg_ref[:, c * D_MODEL:(c + 1) * D_MODEL]
            gate_ref[sl, c * D_MODEL:(c + 1) * D_MODEL] = _sigmoid(z).astype(BF16)
        vr_ref[sl, :] = jnp.dot(h, wb_ref[:, :RET_V_W], preferred_element_type=F32).astype(BF16)


def _in_proj(x2d, seq_len, p, tables):
    n_tok = x2d.shape[0]
    tm = TOKEN_TILE
    n_tiles = n_tok // tm
    tiles_per_seq = seq_len // tm

    def tok(w):
        return pl.BlockSpec((tm, w), lambda i: (i, 0))

    def pos_rows(w):
        return pl.BlockSpec((tm, w), lambda i: (i % tiles_per_seq, 0))

    def pos_lanes(r):
        return pl.BlockSpec((r, tm), lambda i: (0, i % tiles_per_seq))

    out_shape = (
        jax.ShapeDtypeStruct((n_tiles, ATTN_Q_W, tm), BF16),
        jax.ShapeDtypeStruct((n_tiles, ATTN_KV_W, tm), BF16),
        jax.ShapeDtypeStruct((n_tok, ATTN_KV_W), BF16),
        jax.ShapeDtypeStruct((n_tok, RET_QK_W), BF16),
        jax.ShapeDtypeStruct((n_tok, RET_QK_W), BF16),
        jax.ShapeDtypeStruct((n_tok, RET_V_W), BF16),
        jax.ShapeDtypeStruct((n_tok, RET_V_W), BF16),
        jax.ShapeDtypeStruct((n_tok, 2 * D_MODEL), BF16),
    )
    out_specs = (
        pl.BlockSpec((1, ATTN_Q_W, tm), lambda i: (i, 0, 0)),
        pl.BlockSpec((1, ATTN_KV_W, tm), lambda i: (i, 0, 0)),
        tok(ATTN_KV_W), tok(RET_QK_W), tok(RET_QK_W), tok(RET_V_W), tok(RET_V_W), tok(2 * D_MODEL),
    )
    in_specs = [
        tok(D_MODEL),
        _const_spec((1, D_MODEL)),
        _const_spec(p["w_t"].shape),
        _const_spec(p["w_s"].shape),
        _const_spec(p["w_b"].shape),
        _const_spec((1, 2 * D_MODEL)),
        _const_spec((ATTN_HEAD_DIM, tm)),
        _const_spec((1, ATTN_KV_W)),
        _const_spec((1, RET_V_W)),
        pos_lanes(ATTN_HEAD_DIM), pos_rows(ATTN_KV_W), pos_rows(RET_KEY_DIM),
    ]
    return pl.pallas_call(
        _in_proj_kernel,
        grid=(n_tiles,),
        in_specs=in_specs,
        out_specs=out_specs,
        out_shape=out_shape,
        compiler_params=pltpu.CompilerParams(
            dimension_semantics=("arbitrary",), vmem_limit_bytes=V7X_VMEM_LIMIT),
        name="in_proj",
    )(x2d, p["g_mix"], p["w_t"], p["w_s"], p["w_b"], p["b_gate"], p["gq"], p["gk"], p["ret_gain"], *tables)


def _attn_kernel(qt_ref, qnext_ref, k_ref, knext_ref, vt_ref, o_ref, qpad_ref, s_ref, mprev_ref, mcur_ref, acc_ref):
    n_tiles, tq = qt_ref.shape[0], qt_ref.shape[2]
    n_kv, tk = vt_ref.shape[1], vt_ref.shape[3]
    chains = [(t, hd) for t in range(n_tiles) for hd in range(ATTN_HEADS)]
    zeros = jnp.zeros((ATTN_HEAD_DIM, tq), BF16)
    ones = jnp.ones((ONES_ROWS, tk), BF16)
    neg_inf = jnp.full((1, tq), -jnp.inf, F32)

    def v_ext(j):
        return [jnp.concatenate([vt_ref[0, j, g * ATTN_HEAD_DIM:(g + 1) * ATTN_HEAD_DIM, :], ones], axis=0)
                for g in range(ATTN_KV_HEADS)]

    def load_query(c, ref):
        t, hd = chains[c]
        qh = ref[t, hd * ATTN_HEAD_DIM:(hd + 1) * ATTN_HEAD_DIM, :]
        qpad_ref[c] = jnp.concatenate([qh, zeros] if hd < ATTN_GROUP else [zeros, qh], axis=0)

    def probs(c):
        m_cur = mcur_ref[c]
        alpha = jnp.exp2(mprev_ref[c] - m_cur)
        return alpha, jnp.exp2(s_ref[c] - m_cur).astype(BF16), m_cur

    def stage_scores(c, kb, m_cur):
        s = jnp.dot(kb, qpad_ref[c], preferred_element_type=F32)
        s_ref[c] = s
        mprev_ref[c] = m_cur
        mcur_ref[c] = jnp.maximum(m_cur, jnp.max(s, axis=0, keepdims=True))

    def accumulate(c, alpha, pt, vext):
        pv = jnp.dot(vext[chains[c][1] // ATTN_GROUP], pt, preferred_element_type=F32)
        acc_ref[c] = alpha * acc_ref[c] + pv

    @pl.when((pl.program_id(0) == 0) & (pl.program_id(1) == 0))
    def _first_step():
        kb0 = k_ref[0, 0:tk, :]
        for c in range(len(chains)):
            load_query(c, qt_ref)
            stage_scores(c, kb0, neg_inf)

    acc_ref[...] = jnp.zeros(acc_ref.shape, F32)

    def body(j, carry):
        start = pl.multiple_of((j + 1) * tk, tk)
        kb_next = k_ref[0, pl.ds(start, tk), :]
        vext = v_ext(j)
        for c in range(len(chains)):
            alpha, pt, m_cur = probs(c)
            stage_scores(c, kb_next, m_cur)
            accumulate(c, alpha, pt, vext)
        return carry

    lax.fori_loop(0, n_kv - 1, body, 0)

    vext = v_ext(n_kv - 1)
    kb0 = knext_ref[0]
    for c, (t, hd) in enumerate(chains):
        alpha, pt, _ = probs(c)
        load_query(c, qnext_ref)
        stage_scores(c, kb0, neg_inf)
        accumulate(c, alpha, pt, vext)
        acc = acc_ref[c]
        denom = acc[ATTN_HEAD_DIM:ATTN_HEAD_DIM + 1, :]
        o_ref[t, hd * ATTN_HEAD_DIM:(hd + 1) * ATTN_HEAD_DIM, :] = (acc[:ATTN_HEAD_DIM] / denom).astype(BF16)


def _attention(qt, k, vt, batch, seq_len):
    tq = qt.shape[2]
    n_q = seq_len // tq
    n_tiles = ATTN_Q_TILES
    n_steps = n_q // n_tiles
    n_chains = n_tiles * ATTN_HEADS
    k3 = k.reshape(batch, seq_len, ATTN_KV_W)
    vt4 = vt.reshape(batch, n_q, ATTN_KV_W, tq)
    q_block = (n_tiles, ATTN_Q_W, tq)

    def next_step(b, i):
        return jnp.minimum(b * n_steps + i + 1, batch * n_steps - 1)

    return pl.pallas_call(
        _attn_kernel,
        grid=(batch, n_steps),
        in_specs=[
            pl.BlockSpec(q_block, lambda b, i: (b * n_steps + i, 0, 0)),
            pl.BlockSpec(q_block, lambda b, i: (next_step(b, i), 0, 0)),
            pl.BlockSpec((1, seq_len, ATTN_KV_W), lambda b, i: (b, 0, 0)),
            pl.BlockSpec((1, tq, ATTN_KV_W), lambda b, i: (next_step(b, i) // n_steps, 0, 0)),
            pl.BlockSpec((1, n_q, ATTN_KV_W, tq), lambda b, i: (b, 0, 0, 0)),
        ],
        out_specs=pl.BlockSpec(q_block, lambda b, i: (b * n_steps + i, 0, 0)),
        out_shape=jax.ShapeDtypeStruct(qt.shape, BF16),
        scratch_shapes=[
            pltpu.VMEM((n_chains, 2 * ATTN_HEAD_DIM, tq), BF16),
            pltpu.VMEM((n_chains, tq, tq), F32),
            pltpu.VMEM((n_chains, 1, tq), F32),
            pltpu.VMEM((n_chains, 1, tq), F32),
            pltpu.VMEM((n_chains, ATTN_HEAD_DIM + ONES_ROWS, tq), F32),
        ],
        compiler_params=pltpu.CompilerParams(
            dimension_semantics=("arbitrary", "arbitrary"), vmem_limit_bytes=V7X_VMEM_LIMIT),
        name="attn",
    )(qt, qt, k3, k3, vt4)


def _ret_kernel(lg_ref, q_ref, k_ref, v_ref, g_ref, o_ref,
                sf_ref, sb_ref, sball_ref, dmask_ref, qdf_ref, qdb_ref, kdf_ref, kdb_ref):
    b = pl.program_id(0)
    sweep = pl.program_id(1)
    i = pl.program_id(2)
    n_blk = pl.num_programs(2)
    C = RET_CHUNK
    n_sub = k_ref.shape[1] // C
    dk, dv = RET_KEY_DIM, RET_VALUE_DIM

    @pl.when((b == 0) & (sweep == 0) & (i == 0))
    def _decay_tables():
        diff = (lax.broadcasted_iota(jnp.int32, (C, C), 0)
                - lax.broadcasted_iota(jnp.int32, (C, C), 1)).astype(F32)
        row = lax.broadcasted_iota(jnp.int32, (C, dk), 0).astype(F32)
        for hd in range(RET_HEADS):
            lgf, lgb = lg_ref[0, hd], lg_ref[1, hd]
            dmask_ref[hd] = jnp.exp(jnp.where(diff >= 0, lgf * diff, -lgb * diff))
            qdf_ref[hd] = jnp.exp(lgf * (row + 1.0)).astype(BF16)
            qdb_ref[hd] = jnp.exp(lgb * (C - row)).astype(BF16)
            kdf_ref[hd] = jnp.exp(lgf * (C - 1.0 - row)).astype(BF16)
            kdb_ref[hd] = jnp.exp(lgb * row).astype(BF16)

    def decayed_kv(hd, rows, kdec_ref):
        kd = k_ref[0, rows, hd * dk:(hd + 1) * dk] * kdec_ref[hd]
        return lax.dot_general(kd, v_ref[0, rows, hd * dv:(hd + 1) * dv], TN_DIMS, preferred_element_type=F32)

    def chunk_decay(lg):
        return jnp.exp(jnp.full((1, dv), lg * C, F32))

    @pl.when(sweep == 0)
    def _right_to_left():
        @pl.when(i == 0)
        def _():
            sb_ref[...] = jnp.zeros_like(sb_ref)

        blk = n_blk - 1 - i
        for c in reversed(range(n_sub)):
            rows = slice(c * C, (c + 1) * C)
            for hd in range(RET_HEADS):
                state = sb_ref[hd]
                sball_ref[blk * n_sub + c, hd] = state.astype(BF16)
                sb_ref[hd] = chunk_decay(lg_ref[1, hd]) * state + decayed_kv(hd, rows, kdb_ref)

    @pl.when(sweep == 1)
    def _left_to_right():
        @pl.when(i == 0)
        def _():
            sf_ref[...] = jnp.zeros_like(sf_ref)

        for c in range(n_sub):
            rows = slice(c * C, (c + 1) * C)
            for hd in range(RET_HEADS):
                q = q_ref[0, rows, hd * dk:(hd + 1) * dk]
                k = k_ref[0, rows, hd * dk:(hd + 1) * dk]
                v = v_ref[0, rows, hd * dv:(hd + 1) * dv]
                state = sf_ref[hd]
                scores = lax.dot_general(q, k, NT_DIMS, preferred_element_type=F32) * dmask_ref[hd]
                lhs = jnp.concatenate(
                    [scores.astype(BF16), q * qdf_ref[hd], q * qdb_ref[hd]], axis=1)
                rhs = jnp.concatenate([v, state.astype(BF16), sball_ref[i * n_sub + c, hd]], axis=0)
                y = jnp.dot(lhs, rhs, preferred_element_type=F32)
                sf_ref[hd] = chunk_decay(lg_ref[0, hd]) * state + decayed_kv(hd, rows, kdf_ref)

                mean = jnp.mean(y, axis=-1, keepdims=True)
                d = y - mean
                var = jnp.mean(d * d, axis=-1, keepdims=True)
                yn = d * lax.rsqrt(var + EPS)
                gate = g_ref[0, rows, hd * dv:(hd + 1) * dv].astype(F32)
                o_ref[0, rows, hd * dv:(hd + 1) * dv] = (gate * yn).astype(BF16)


def _retention(qr, kr, vr, gr, log_gamma, batch, seq_len):
    tb = RET_CHUNK * RET_CHUNKS_PER_STEP
    n_blk = seq_len // tb
    n_chunks = seq_len // RET_CHUNK
    shp = lambda a: a.reshape(batch, seq_len, a.shape[-1])

    def both(b, p, i):
        return (b, jnp.where(p == 0, n_blk - 1 - i, i), 0)

    def fwd_only(b, p, i):
        return (b, jnp.where(p == 0, 0, i), 0)

    out = pl.pallas_call(
        _ret_kernel,
        grid=(batch, 2, n_blk),
        in_specs=[
            pl.BlockSpec(memory_space=pltpu.SMEM),
            pl.BlockSpec((1, tb, RET_QK_W), fwd_only),
            pl.BlockSpec((1, tb, RET_QK_W), both),
            pl.BlockSpec((1, tb, RET_V_W), both),
            pl.BlockSpec((1, tb, RET_V_W), fwd_only),
        ],
        out_specs=pl.BlockSpec((1, tb, RET_V_W), fwd_only),
        out_shape=jax.ShapeDtypeStruct((batch, seq_len, RET_V_W), BF16),
        scratch_shapes=[
            pltpu.VMEM((RET_HEADS, RET_KEY_DIM, RET_VALUE_DIM), F32),
            pltpu.VMEM((RET_HEADS, RET_KEY_DIM, RET_VALUE_DIM), F32),
            pltpu.VMEM((n_chunks, RET_HEADS, RET_KEY_DIM, RET_VALUE_DIM), BF16),
            pltpu.VMEM((RET_HEADS, RET_CHUNK, RET_CHUNK), F32),
            pltpu.VMEM((RET_HEADS, RET_CHUNK, RET_KEY_DIM), BF16),
            pltpu.VMEM((RET_HEADS, RET_CHUNK, RET_KEY_DIM), BF16),
            pltpu.VMEM((RET_HEADS, RET_CHUNK, RET_KEY_DIM), BF16),
            pltpu.VMEM((RET_HEADS, RET_CHUNK, RET_KEY_DIM), BF16),
        ],
        compiler_params=pltpu.CompilerParams(
            dimension_semantics=("arbitrary",) * 3, vmem_limit_bytes=V7X_VMEM_LIMIT),
        name="retention",
    )(log_gamma, shp(qr), shp(kr), shp(vr), shp(gr))
    return out.reshape(batch * seq_len, RET_V_W)


def _post_kernel(x_ref, at_ref, ret_ref, gate_ref, wa_ref, wr_ref, wo_ref,
                 gffn_ref, win_ref, wout_ref, gfin_ref, o_ref):
    rows = x_ref.shape[0] // ROW_CHAINS
    chains = [slice(rc * rows, (rc + 1) * rows) for rc in range(ROW_CHAINS)]
    x1s = []
    for sl in chains:
        r = jnp.dot(ret_ref[sl, :], wr_ref[...], preferred_element_type=F32)
        a = lax.dot_general(at_ref[0, :, sl], wa_ref[...], TN_DIMS, preferred_element_type=F32)
        ga = gate_ref[sl, :D_MODEL].astype(F32)
        gr = gate_ref[sl, D_MODEL:].astype(F32)
        mixed = (ga * a + gr * r).astype(BF16)
        x1s.append(x_ref[sl, :] + jnp.dot(mixed, wo_ref[...], preferred_element_type=F32))
    for sl, x1 in zip(chains, x1s):
        h = (_rms(x1) * gffn_ref[...]).astype(BF16)
        acc = x1
        for c0, c1 in FFN_CHUNKS:
            gt = jnp.dot(h, win_ref[:, c0:c1], preferred_element_type=F32)
            up = jnp.dot(h, win_ref[:, D_FF + c0:D_FF + c1], preferred_element_type=F32)
            act = (gt * _sigmoid(gt) * up).astype(BF16)
            acc = acc + jnp.dot(act, wout_ref[c0:c1, :], preferred_element_type=F32)
        o_ref[sl, :] = _rms(acc) * gfin_ref[...]


def _post(x2d, attn_t, ret, gates, p):
    n_tok = x2d.shape[0]
    tm = TOKEN_TILE
    tok = lambda w: pl.BlockSpec((tm, w), lambda i: (i, 0))
    return pl.pallas_call(
        _post_kernel,
        grid=(n_tok // tm,),
        in_specs=[
            tok(D_MODEL),
            pl.BlockSpec((1, ATTN_Q_W, tm), lambda i: (i, 0, 0)),
            tok(RET_V_W), tok(2 * D_MODEL),
            _const_spec((ATTN_Q_W, D_MODEL)),
            _const_spec((RET_V_W, D_MODEL)),
            _const_spec((D_MODEL, D_MODEL)),
            _const_spec((1, D_MODEL)),
            _const_spec((D_MODEL, 2 * D_FF)),
            _const_spec((D_FF, D_MODEL)),
            _const_spec((1, D_MODEL)),
        ],
        out_specs=tok(D_MODEL),
        out_shape=jax.ShapeDtypeStruct((n_tok, D_MODEL), F32),
        compiler_params=pltpu.CompilerParams(
            dimension_semantics=("arbitrary",), vmem_limit_bytes=V7X_VMEM_LIMIT),
        name="post",
    )(x2d, attn_t, ret, gates, p["w_a"], p["w_r"], p["w_o"],
      p["g_ffn"], p["w_ffn_in"], p["w_ffn_out"], p["g_fin"])


def _deinterleave(w, n_heads, head_dim):
    w4 = w.reshape(w.shape[0], n_heads, head_dim // 2, 2)
    return jnp.concatenate([w4[..., 0], w4[..., 1]], axis=-1).reshape(w.shape[0], n_heads * head_dim)


def _prepare_params(norm_mix, w_in, b_gate, q_norm, k_norm, ret_decay_fwd, ret_decay_bwd, ret_norm,
                    w_branch_attn, w_branch_ret, w_out, norm_ffn, w_ffn_in, w_ffn_out, norm_final):
    widths = [ATTN_Q_W, ATTN_KV_W, ATTN_KV_W, RET_QK_W, RET_QK_W]
    offs = np.concatenate([[0], np.cumsum(widths)])
    w_qa, w_ka, w_va, w_qr, w_kr = [w_in[:, offs[j]:offs[j + 1]] for j in range(len(widths))]
    w_qa = _deinterleave(w_qa, ATTN_HEADS, ATTN_HEAD_DIM)
    w_ka = _deinterleave(w_ka, ATTN_KV_HEADS, ATTN_HEAD_DIM)
    w_qr = _deinterleave(w_qr, RET_HEADS, RET_KEY_DIM)
    w_kr = _deinterleave(w_kr, RET_HEADS, RET_KEY_DIM)
    q_scale = ATTN_HEAD_DIM ** -0.5 * math.log2(math.e)
    gq = _deinterleave(q_norm.reshape(1, -1), 1, ATTN_HEAD_DIM).reshape(-1) * q_scale
    gk = _deinterleave(k_norm.reshape(1, -1), 1, ATTN_HEAD_DIM).reshape(-1)
    return {
        "g_mix": norm_mix.reshape(1, D_MODEL),
        "w_t": jnp.concatenate([w_qa, w_va], axis=1).T.astype(BF16),
        "w_s": jnp.concatenate([w_ka, w_qr, w_kr], axis=1).astype(BF16),
        "w_b": w_in[:, offs[-1]:].astype(BF16),
        "b_gate": b_gate.reshape(1, 2 * D_MODEL),
        "gq": jnp.broadcast_to(gq.astype(F32)[:, None], (ATTN_HEAD_DIM, TOKEN_TILE)),
        "gk": jnp.tile(gk, ATTN_KV_HEADS).reshape(1, ATTN_KV_W),
        "log_gamma": jnp.stack([jax.nn.log_sigmoid(ret_decay_fwd.astype(F32)),
                                jax.nn.log_sigmoid(ret_decay_bwd.astype(F32))]),
        "ret_gain": ret_norm.reshape(1, RET_V_W),
        "w_a": w_branch_attn.astype(BF16),
        "w_r": w_branch_ret.astype(BF16),
        "w_o": w_out.astype(BF16),
        "g_ffn": norm_ffn.reshape(1, D_MODEL),
        "w_ffn_in": w_ffn_in.astype(BF16),
        "w_ffn_out": w_ffn_out.astype(BF16),
        "g_fin": norm_final.reshape(1, D_MODEL),
    }


@functools.lru_cache(maxsize=None)
def _rope_tables(seq_len):
    pos = np.arange(seq_len)
    row = (pos // GRID_W).astype(np.float64)[:, None]
    col = (pos % GRID_W).astype(np.float64)[:, None]

    def cos_sin(head_dim):
        n_freq = head_dim // 4
        inv_freq = ROPE_THETA ** (-np.arange(n_freq, dtype=np.float64) / n_freq)
        ang = np.concatenate([row * inv_freq, col * inv_freq], axis=-1)
        return np.cos(ang).astype(np.float32), np.sin(ang).astype(np.float32)

    ca, sa = cos_sin(ATTN_HEAD_DIM)
    cr, sr = cos_sin(RET_KEY_DIM)
    return (np.ascontiguousarray(np.concatenate([ca, sa], axis=-1).T),
            np.concatenate([ca, sa, ca, sa], axis=-1),
            np.concatenate([cr, sr], axis=-1))


def _trunk(x, p):
    batch, seq_len, _ = x.shape
    x2d = x.reshape(batch * seq_len, D_MODEL)
    qt, vt, k, qr, kr, vr, gr, gates = _in_proj(x2d, seq_len, p, _rope_tables(seq_len))
    attn_t = _attention(qt, k, vt, batch, seq_len)
    ret = _retention(qr, kr, vr, gr, p["log_gamma"], batch, seq_len)
    return _post(x2d, attn_t, ret, gates, p).reshape(batch, seq_len, D_MODEL)


def kernel(x_prompt, x_sample, norm_mix, w_in, b_gate, q_norm, k_norm, ret_decay_fwd, ret_decay_bwd,
           ret_norm, w_branch_attn, w_branch_ret, w_out, norm_ffn, w_ffn_in, w_ffn_out, norm_final):
    params = _prepare_params(norm_mix[0], w_in[0], b_gate[0], q_norm[0], k_norm[0], ret_decay_fwd[0],
                             ret_decay_bwd[0], ret_norm[0], w_branch_attn[0], w_branch_ret[0], w_out[0],
                             norm_ffn[0], w_ffn_in[0], w_ffn_out[0], norm_final)
    return _trunk(x_prompt, params), _trunk(x_sample, params)
```

```python
import functools
import math

import jax
import jax.numpy as jnp
import numpy as np
from jax import lax
from jax.experimental import pallas as pl
from jax.experimental.pallas import tpu as pltpu

F32 = jnp.float32
BF16 = jnp.bfloat16

D_MODEL = 1024
GRID_W = 64
ATTN_HEADS = 8
ATTN_KV_HEADS = 2
ATTN_GROUP = ATTN_HEADS // ATTN_KV_HEADS
ATTN_HEAD_DIM = 64
RET_HEADS = 4
RET_KEY_DIM = 128
RET_VALUE_DIM = 256
ATTN_Q_W = ATTN_HEADS * ATTN_HEAD_DIM
ATTN_KV_W = ATTN_KV_HEADS * ATTN_HEAD_DIM
RET_QK_W = RET_HEADS * RET_KEY_DIM
RET_V_W = RET_HEADS * RET_VALUE_DIM
D_FF = 2816
ROPE_THETA = 10000.0
EPS = 1e-6

TOKEN_TILE = 512
IN_PROJ_TILES = 2
ATTN_Q_TILES = 2
RET_CHUNK = 256
RET_CHUNKS_PER_STEP = 4
FFN_CHUNKS = ((0, 1536), (1536, 2816))
CHAIN_ROWS = 256
ONES_ROWS = 64
V7X_VMEM_LIMIT = 56 * 1024 * 1024

NT_DIMS = (((1,), (1,)), ((), ()))
TN_DIMS = (((0,), (0,)), ((), ()))


def _const_spec(shape):
    nd = len(shape)
    return pl.BlockSpec(shape, lambda *_: (0,) * nd, pipeline_mode=pl.Buffered(1))


def _rms(x):
    return x * lax.rsqrt(jnp.mean(x * x, axis=-1, keepdims=True) + EPS)


def _sigmoid(x):
    return 1.0 / (1.0 + jnp.exp(-x))


def _in_proj_kernel(x_ref, gmix_ref, wt_ref, ws_ref, wb_ref, bg_ref, gq_ref, gk_ref, rgain_ref,
                    tq_ref, tk_ref, tr_ref,
                    qt_ref, vt_ref, k_ref, qr_ref, kr_ref, vr_ref, gr_ref, gate_ref):
    rows = CHAIN_ROWS
    half = ATTN_HEAD_DIM // 2
    lane = lax.broadcasted_iota(jnp.int32, (rows, ATTN_KV_W), 1)
    lo = lane < ATTN_HEAD_DIM
    first_half = (lane % ATTN_HEAD_DIM) < half

    for rc in range(x_ref.shape[0] // rows):
        sl = slice(rc * rows, (rc + 1) * rows)
        tile, t0 = divmod(rc * rows, TOKEN_TILE)
        tl = slice(t0, t0 + rows)
        h = (_rms(x_ref[sl, :]) * gmix_ref[...]).astype(BF16)

        t = lax.dot_general(wt_ref[...], h, NT_DIMS, preferred_element_type=F32)
        cat, sat = tq_ref[:half, sl], tq_ref[half:, sl]
        gq = gq_ref[...]
        for hd in range(ATTN_HEADS):
            r0 = hd * ATTN_HEAD_DIM
            blk = t[r0:r0 + ATTN_HEAD_DIM]
            inv = lax.rsqrt(jnp.mean(blk * blk, axis=0, keepdims=True) + EPS)
            xn = blk * inv * gq
            x0, x1 = xn[:half], xn[half:]
            qt_ref[tile, r0:r0 + half, tl] = (x0 * cat - x1 * sat).astype(BF16)
            qt_ref[tile, r0 + half:r0 + ATTN_HEAD_DIM, tl] = (x0 * sat + x1 * cat).astype(BF16)
        vt_ref[tile, :, tl] = t[ATTN_Q_W:ATTN_Q_W + ATTN_KV_W].astype(BF16)

        o = 0
        kf = jnp.dot(h, ws_ref[:, o:o + ATTN_KV_W], preferred_element_type=F32)
        o += ATTN_KV_W
        k2 = kf * kf
        s_lo = jnp.sum(jnp.where(lo, k2, 0.0), axis=-1, keepdims=True)
        s_hi = jnp.sum(jnp.where(lo, 0.0, k2), axis=-1, keepdims=True)
        inv = jnp.where(lo, lax.rsqrt(s_lo * (1.0 / ATTN_HEAD_DIM) + EPS),
                        lax.rsqrt(s_hi * (1.0 / ATTN_HEAD_DIM) + EPS))
        kn = kf * inv * gk_ref[...]
        tab = tk_ref[sl, :]
        tab_rot = pltpu.roll(tab, half, 1)
        ck = jnp.where(first_half, tab, tab_rot)
        sk = jnp.where(first_half, -tab_rot, tab)
        partner = jnp.where(first_half, pltpu.roll(kn, ATTN_KV_W - half, 1), pltpu.roll(kn, half, 1))
        k_ref[sl, :] = (kn * ck + partner * sk).astype(BF16)

        tab = tr_ref[sl, :]
        tab_rot = pltpu.roll(tab, RET_KEY_DIM // 2, 1)
        cr = jnp.where(lo, tab, tab_rot)
        sr = jnp.where(lo, -tab_rot, tab)
        for dst, scale in ((qr_ref, RET_KEY_DIM ** -0.5), (kr_ref, None)):
            y = jnp.dot(h, ws_ref[:, o:o + RET_QK_W], preferred_element_type=F32)
            o += RET_QK_W
            for hd in range(RET_HEADS):
                c0 = hd * RET_KEY_DIM
                xh = y[:, c0:c0 + RET_KEY_DIM]
                r = xh * cr + pltpu.roll(xh, RET_KEY_DIM // 2, 1) * sr
                if scale is not None:
                    r = r * scale
                dst[sl, c0:c0 + RET_KEY_DIM] = r.astype(BF16)

        gl = jnp.dot(h, wb_ref[:, RET_V_W:2 * RET_V_W], preferred_element_type=F32)
        gr_ref[sl, :] = (gl * _sigmoid(gl) * rgain_ref[...]).astype(BF16)
        for c in range(2):
            c0 = 2 * RET_V_W + c * D_MODEL
            z = jnp.dot(h, wb_ref[:, c0:c0 + D_MODEL], preferred_element_type=F32)
            z = z + bg_ref[:, c * D_MODEL:(c + 1) * D_MODEL]
            gate_ref[sl, c * D_MODEL:(c + 1) * D_MODEL] = _sigmoid(z).astype(BF16)
        vr_ref[sl, :] = jnp.dot(h, wb_ref[:, :RET_V_W], preferred_element_type=F32).astype(BF16)


def _in_proj(x2d, seq_len, p, tables):
    n_tok = x2d.shape[0]
    tm = IN_PROJ_TILES * TOKEN_TILE
    n_tiles = n_tok // TOKEN_TILE
    tiles_per_seq = seq_len // tm

    def tok(w):
        return pl.BlockSpec((tm, w), lambda i: (i, 0))

    def pos_rows(w):
        return pl.BlockSpec((tm, w), lambda i: (i % tiles_per_seq, 0))

    def pos_lanes(r):
        return pl.BlockSpec((r, tm), lambda i: (0, i % tiles_per_seq))

    out_shape = (
        jax.ShapeDtypeStruct((n_tiles, ATTN_Q_W, TOKEN_TILE), BF16),
        jax.ShapeDtypeStruct((n_tiles, ATTN_KV_W, TOKEN_TILE), BF16),
        jax.ShapeDtypeStruct((n_tok, ATTN_KV_W), BF16),
        jax.ShapeDtypeStruct((n_tok, RET_QK_W), BF16),
        jax.ShapeDtypeStruct((n_tok, RET_QK_W), BF16),
        jax.ShapeDtypeStruct((n_tok, RET_V_W), BF16),
        jax.ShapeDtypeStruct((n_tok, RET_V_W), BF16),
        jax.ShapeDtypeStruct((n_tok, 2 * D_MODEL), BF16),
    )
    out_specs = (
        pl.BlockSpec((IN_PROJ_TILES, ATTN_Q_W, TOKEN_TILE), lambda i: (i, 0, 0)),
        pl.BlockSpec((IN_PROJ_TILES, ATTN_KV_W, TOKEN_TILE), lambda i: (i, 0, 0)),
        tok(ATTN_KV_W), tok(RET_QK_W), tok(RET_QK_W), tok(RET_V_W), tok(RET_V_W), tok(2 * D_MODEL),
    )
    in_specs = [
        tok(D_MODEL),
        _const_spec((1, D_MODEL)),
        _const_spec(p["w_t"].shape),
        _const_spec(p["w_s"].shape),
        _const_spec(p["w_b"].shape),
        _const_spec((1, 2 * D_MODEL)),
        _const_spec((ATTN_HEAD_DIM, CHAIN_ROWS)),
        _const_spec((1, ATTN_KV_W)),
        _const_spec((1, RET_V_W)),
        pos_lanes(ATTN_HEAD_DIM), pos_rows(ATTN_KV_W), pos_rows(RET_KEY_DIM),
    ]
    return pl.pallas_call(
        _in_proj_kernel,
        grid=(n_tok // tm,),
        in_specs=in_specs,
        out_specs=out_specs,
        out_shape=out_shape,
        compiler_params=pltpu.CompilerParams(
            dimension_semantics=("arbitrary",), vmem_limit_bytes=V7X_VMEM_LIMIT),
        name="in_proj",
    )(x2d, p["g_mix"], p["w_t"], p["w_s"], p["w_b"], p["b_gate"], p["gq"], p["gk"], p["ret_gain"], *tables)


def _attn_kernel(qt_ref, qnext_ref, k_ref, knext_ref, vt_ref, o_ref, qpad_ref, s_ref, mprev_ref, mcur_ref, acc_ref):
    n_tiles, tq = qt_ref.shape[0], qt_ref.shape[2]
    n_kv, tk = vt_ref.shape[1], vt_ref.shape[3]
    chains = [(t, hd) for t in range(n_tiles) for hd in range(ATTN_HEADS)]
    zeros = jnp.zeros((ATTN_HEAD_DIM, tq), BF16)
    ones = jnp.ones((ONES_ROWS, tk), BF16)
    neg_inf = jnp.full((1, tq), -jnp.inf, F32)

    def v_ext(j):
        return [jnp.concatenate([vt_ref[0, j, g * ATTN_HEAD_DIM:(g + 1) * ATTN_HEAD_DIM, :], ones], axis=0)
                for g in range(ATTN_KV_HEADS)]

    def load_query(c, ref):
        t, hd = chains[c]
        qh = ref[t, hd * ATTN_HEAD_DIM:(hd + 1) * ATTN_HEAD_DIM, :]
        qpad_ref[c] = jnp.concatenate([qh, zeros] if hd < ATTN_GROUP else [zeros, qh], axis=0)

    def probs(c):
        m_cur = mcur_ref[c]
        alpha = jnp.exp2(mprev_ref[c] - m_cur)
        return alpha, jnp.exp2(s_ref[c] - m_cur).astype(BF16), m_cur

    def stage_scores(c, kb, m_cur):
        s = jnp.dot(kb, qpad_ref[c], preferred_element_type=F32)
        s_ref[c] = s
        mprev_ref[c] = m_cur
        mcur_ref[c] = jnp.maximum(m_cur, jnp.max(s, axis=0, keepdims=True))

    def accumulate(c, alpha, pt, vext):
        pv = jnp.dot(vext[chains[c][1] // ATTN_GROUP], pt, preferred_element_type=F32)
        acc_ref[c] = alpha * acc_ref[c] + pv

    @pl.when((pl.program_id(0) == 0) & (pl.program_id(1) == 0))
    def _first_step():
        kb0 = k_ref[0, 0:tk, :]
        for c in range(len(chains)):
            load_query(c, qt_ref)
            stage_scores(c, kb0, neg_inf)

    acc_ref[...] = jnp.zeros(acc_ref.shape, F32)

    def body(j, carry):
        start = pl.multiple_of((j + 1) * tk, tk)
        kb_next = k_ref[0, pl.ds(start, tk), :]
        vext = v_ext(j)
        for c in range(len(chains)):
            alpha, pt, m_cur = probs(c)
            stage_scores(c, kb_next, m_cur)
            accumulate(c, alpha, pt, vext)
        return carry

    lax.fori_loop(0, n_kv - 1, body, 0)

    vext = v_ext(n_kv - 1)
    kb0 = knext_ref[0]
    for c, (t, hd) in enumerate(chains):
        alpha, pt, _ = probs(c)
        load_query(c, qnext_ref)
        stage_scores(c, kb0, neg_inf)
        accumulate(c, alpha, pt, vext)
        acc = acc_ref[c]
        denom = acc[ATTN_HEAD_DIM:ATTN_HEAD_DIM + 1, :]
        o_ref[t, hd * ATTN_HEAD_DIM:(hd + 1) * ATTN_HEAD_DIM, :] = (acc[:ATTN_HEAD_DIM] / denom).astype(BF16)


def _attention(qt, k, vt, batch, seq_len):
    tq = qt.shape[2]
    n_q = seq_len // tq
    n_tiles = ATTN_Q_TILES
    n_steps = n_q // n_tiles
    n_chains = n_tiles * ATTN_HEADS
    k3 = k.reshape(batch, seq_len, ATTN_KV_W)
    vt4 = vt.reshape(batch, n_q, ATTN_KV_W, tq)
    q_block = (n_tiles, ATTN_Q_W, tq)

    def next_step(b, i):
        return jnp.minimum(b * n_steps + i + 1, batch * n_steps - 1)

    return pl.pallas_call(
        _attn_kernel,
        grid=(batch, n_steps),
        in_specs=[
            pl.BlockSpec(q_block, lambda b, i: (b * n_steps + i, 0, 0)),
            pl.BlockSpec(q_block, lambda b, i: (next_step(b, i), 0, 0)),
            pl.BlockSpec((1, seq_len, ATTN_KV_W), lambda b, i: (b, 0, 0)),
            pl.BlockSpec((1, tq, ATTN_KV_W), lambda b, i: (next_step(b, i) // n_steps, 0, 0)),
            pl.BlockSpec((1, n_q, ATTN_KV_W, tq), lambda b, i: (b, 0, 0, 0)),
        ],
        out_specs=pl.BlockSpec(q_block, lambda b, i: (b * n_steps + i, 0, 0)),
        out_shape=jax.ShapeDtypeStruct(qt.shape, BF16),
        scratch_shapes=[
            pltpu.VMEM((n_chains, 2 * ATTN_HEAD_DIM, tq), BF16),
            pltpu.VMEM((n_chains, tq, tq), F32),
            pltpu.VMEM((n_chains, 1, tq), F32),
            pltpu.VMEM((n_chains, 1, tq), F32),
            pltpu.VMEM((n_chains, ATTN_HEAD_DIM + ONES_ROWS, tq), F32),
        ],
        compiler_params=pltpu.CompilerParams(
            dimension_semantics=("arbitrary", "arbitrary"), vmem_limit_bytes=V7X_VMEM_LIMIT),
        name="attn",
    )(qt, qt, k3, k3, vt4)


def _ret_kernel(lg_ref, q_ref, k_ref, v_ref, g_ref, o_ref,
                sf_ref, sb_ref, sball_ref, dmask_ref, qdf_ref, qdb_ref, kdf_ref, kdb_ref):
    b = pl.program_id(0)
    sweep = pl.program_id(1)
    i = pl.program_id(2)
    n_blk = pl.num_programs(2)
    C = RET_CHUNK
    n_sub = k_ref.shape[1] // C
    dk, dv = RET_KEY_DIM, RET_VALUE_DIM

    @pl.when((b == 0) & (sweep == 0) & (i == 0))
    def _decay_tables():
        diff = (lax.broadcasted_iota(jnp.int32, (C, C), 0)
                - lax.broadcasted_iota(jnp.int32, (C, C), 1)).astype(F32)
        row = lax.broadcasted_iota(jnp.int32, (C, dk), 0).astype(F32)
        for hd in range(RET_HEADS):
            lgf, lgb = lg_ref[0, hd], lg_ref[1, hd]
            dmask_ref[hd] = jnp.exp(jnp.where(diff >= 0, lgf * diff, -lgb * diff))
            qdf_ref[hd] = jnp.exp(lgf * (row + 1.0)).astype(BF16)
            qdb_ref[hd] = jnp.exp(lgb * (C - row)).astype(BF16)
            kdf_ref[hd] = jnp.exp(lgf * (C - 1.0 - row)).astype(BF16)
            kdb_ref[hd] = jnp.exp(lgb * row).astype(BF16)

    def decayed_kv(hd, rows, kdec_ref):
        kd = k_ref[0, rows, hd * dk:(hd + 1) * dk] * kdec_ref[hd]
        return lax.dot_general(kd, v_ref[0, rows, hd * dv:(hd + 1) * dv], TN_DIMS, preferred_element_type=F32)

    def chunk_decay(lg):
        return jnp.exp(jnp.full((1, dv), lg * C, F32))

    @pl.when(sweep == 0)
    def _right_to_left():
        @pl.when(i == 0)
        def _():
            sb_ref[...] = jnp.zeros_like(sb_ref)

        blk = n_blk - 1 - i
        for c in reversed(range(n_sub)):
            rows = slice(c * C, (c + 1) * C)
            for hd in range(RET_HEADS):
                state = sb_ref[hd]
                sball_ref[blk * n_sub + c, hd] = state.astype(BF16)
                sb_ref[hd] = chunk_decay(lg_ref[1, hd]) * state + decayed_kv(hd, rows, kdb_ref)

    @pl.when(sweep == 1)
    def _left_to_right():
        @pl.when(i == 0)
        def _():
            sf_ref[...] = jnp.zeros_like(sf_ref)

        for c in range(n_sub):
            rows = slice(c * C, (c + 1) * C)
            for hd in range(RET_HEADS):
                q = q_ref[0, rows, hd * dk:(hd + 1) * dk]
                k = k_ref[0, rows, hd * dk:(hd + 1) * dk]
                v = v_ref[0, rows, hd * dv:(hd + 1) * dv]
                state = sf_ref[hd]
                scores = lax.dot_general(q, k, NT_DIMS, preferred_element_type=F32) * dmask_ref[hd]
                lhs = jnp.concatenate(
                    [scores.astype(BF16), q * qdf_ref[hd], q * qdb_ref[hd]], axis=1)
                rhs = jnp.concatenate([v, state.astype(BF16), sball_ref[i * n_sub + c, hd]], axis=0)
                y = jnp.dot(lhs, rhs, preferred_element_type=F32)
                sf_ref[hd] = chunk_decay(lg_ref[0, hd]) * state + decayed_kv(hd, rows, kdf_ref)

                mean = jnp.mean(y, axis=-1, keepdims=True)
                d = y - mean
                var = jnp.mean(d * d, axis=-1, keepdims=True)
                yn = d * lax.rsqrt(var + EPS)
                gate = g_ref[0, rows, hd * dv:(hd + 1) * dv].astype(F32)
                o_ref[0, rows, hd * dv:(hd + 1) * dv] = (gate * yn).astype(BF16)


def _retention(qr, kr, vr, gr, log_gamma, batch, seq_len):
    tb = RET_CHUNK * RET_CHUNKS_PER_STEP
    n_blk = seq_len // tb
    n_chunks = seq_len // RET_CHUNK
    shp = lambda a: a.reshape(batch, seq_len, a.shape[-1])

    def both(b, p, i):
        return (b, jnp.where(p == 0, n_blk - 1 - i, i), 0)

    def fwd_only(b, p, i):
        return (b, jnp.where(p == 0, 0, i), 0)

    out = pl.pallas_call(
        _ret_kernel,
        grid=(batch, 2, n_blk),
        in_specs=[
            pl.BlockSpec(memory_space=pltpu.SMEM),
            pl.BlockSpec((1, tb, RET_QK_W), fwd_only),
            pl.BlockSpec((1, tb, RET_QK_W), both),
            pl.BlockSpec((1, tb, RET_V_W), both),
            pl.BlockSpec((1, tb, RET_V_W), fwd_only),
        ],
        out_specs=pl.BlockSpec((1, tb, RET_V_W), fwd_only),
        out_shape=jax.ShapeDtypeStruct((batch, seq_len, RET_V_W), BF16),
        scratch_shapes=[
            pltpu.VMEM((RET_HEADS, RET_KEY_DIM, RET_VALUE_DIM), F32),
            pltpu.VMEM((RET_HEADS, RET_KEY_DIM, RET_VALUE_DIM), F32),
            pltpu.VMEM((n_chunks, RET_HEADS, RET_KEY_DIM, RET_VALUE_DIM), BF16),
            pltpu.VMEM((RET_HEADS, RET_CHUNK, RET_CHUNK), F32),
            pltpu.VMEM((RET_HEADS, RET_CHUNK, RET_KEY_DIM), BF16),
            pltpu.VMEM((RET_HEADS, RET_CHUNK, RET_KEY_DIM), BF16),
            pltpu.VMEM((RET_HEADS, RET_CHUNK, RET_KEY_DIM), BF16),
            pltpu.VMEM((RET_HEADS, RET_CHUNK, RET_KEY_DIM), BF16),
        ],
        compiler_params=pltpu.CompilerParams(
            dimension_semantics=("arbitrary",) * 3, vmem_limit_bytes=V7X_VMEM_LIMIT),
        name="retention",
    )(log_gamma, shp(qr), shp(kr), shp(vr), shp(gr))
    return out.reshape(batch * seq_len, RET_V_W)


def _post_kernel(x_ref, at_ref, ret_ref, gate_ref, wa_ref, wr_ref, wo_ref,
                 gffn_ref, win_ref, wout_ref, gfin_ref, o_ref):
    rows = CHAIN_ROWS
    chains = [slice(r0, r0 + rows) for r0 in range(0, x_ref.shape[0], rows)]
    x1s = []
    for sl in chains:
        r = jnp.dot(ret_ref[sl, :], wr_ref[...], preferred_element_type=F32)
        a = lax.dot_general(at_ref[0, :, sl], wa_ref[...], TN_DIMS, preferred_element_type=F32)
        ga = gate_ref[sl, :D_MODEL].astype(F32)
        gr = gate_ref[sl, D_MODEL:].astype(F32)
        mixed = (ga * a + gr * r).astype(BF16)
        x1s.append(x_ref[sl, :] + jnp.dot(mixed, wo_ref[...], preferred_element_type=F32))
    for sl, x1 in zip(chains, x1s):
        h = (_rms(x1) * gffn_ref[...]).astype(BF16)
        acc = x1
        for c0, c1 in FFN_CHUNKS:
            gt = jnp.dot(h, win_ref[:, c0:c1], preferred_element_type=F32)
            up = jnp.dot(h, win_ref[:, D_FF + c0:D_FF + c1], preferred_element_type=F32)
            act = (gt * _sigmoid(gt) * up).astype(BF16)
            acc = acc + jnp.dot(act, wout_ref[c0:c1, :], preferred_element_type=F32)
        o_ref[sl, :] = _rms(acc) * gfin_ref[...]


def _post(x2d, attn_t, ret, gates, p):
    n_tok = x2d.shape[0]
    tm = TOKEN_TILE
    tok = lambda w: pl.BlockSpec((tm, w), lambda i: (i, 0))
    return pl.pallas_call(
        _post_kernel,
        grid=(n_tok // tm,),
        in_specs=[
            tok(D_MODEL),
            pl.BlockSpec((1, ATTN_Q_W, tm), lambda i: (i, 0, 0)),
            tok(RET_V_W), tok(2 * D_MODEL),
            _const_spec((ATTN_Q_W, D_MODEL)),
            _const_spec((RET_V_W, D_MODEL)),
            _const_spec((D_MODEL, D_MODEL)),
            _const_spec((1, D_MODEL)),
            _const_spec((D_MODEL, 2 * D_FF)),
            _const_spec((D_FF, D_MODEL)),
            _const_spec((1, D_MODEL)),
        ],
        out_specs=tok(D_MODEL),
        out_shape=jax.ShapeDtypeStruct((n_tok, D_MODEL), F32),
        compiler_params=pltpu.CompilerParams(
            dimension_semantics=("arbitrary",), vmem_limit_bytes=V7X_VMEM_LIMIT),
        name="post",
    )(x2d, attn_t, ret, gates, p["w_a"], p["w_r"], p["w_o"],
      p["g_ffn"], p["w_ffn_in"], p["w_ffn_out"], p["g_fin"])


def _deinterleave(w, n_heads, head_dim):
    w4 = w.reshape(w.shape[0], n_heads, head_dim // 2, 2)
    return jnp.concatenate([w4[..., 0], w4[..., 1]], axis=-1).reshape(w.shape[0], n_heads * head_dim)


def _prepare_params(norm_mix, w_in, b_gate, q_norm, k_norm, ret_decay_fwd, ret_decay_bwd, ret_norm,
                    w_branch_attn, w_branch_ret, w_out, norm_ffn, w_ffn_in, w_ffn_out, norm_final):
    widths = [ATTN_Q_W, ATTN_KV_W, ATTN_KV_W, RET_QK_W, RET_QK_W]
    offs = np.concatenate([[0], np.cumsum(widths)])
    w_qa, w_ka, w_va, w_qr, w_kr = [w_in[:, offs[j]:offs[j + 1]] for j in range(len(widths))]
    w_qa = _deinterleave(w_qa, ATTN_HEADS, ATTN_HEAD_DIM)
    w_ka = _deinterleave(w_ka, ATTN_KV_HEADS, ATTN_HEAD_DIM)
    w_qr = _deinterleave(w_qr, RET_HEADS, RET_KEY_DIM)
    w_kr = _deinterleave(w_kr, RET_HEADS, RET_KEY_DIM)
    q_scale = ATTN_HEAD_DIM ** -0.5 * math.log2(math.e)
    gq = _deinterleave(q_norm.reshape(1, -1), 1, ATTN_HEAD_DIM).reshape(-1) * q_scale
    gk = _deinterleave(k_norm.reshape(1, -1), 1, ATTN_HEAD_DIM).reshape(-1)
    return {
        "g_mix": norm_mix.reshape(1, D_MODEL),
        "w_t": jnp.concatenate([w_qa, w_va], axis=1).T.astype(BF16),
        "w_s": jnp.concatenate([w_ka, w_qr, w_kr], axis=1).astype(BF16),
        "w_b": w_in[:, offs[-1]:].astype(BF16),
        "b_gate": b_gate.reshape(1, 2 * D_MODEL),
        "gq": jnp.broadcast_to(gq.astype(F32)[:, None], (ATTN_HEAD_DIM, CHAIN_ROWS)),
        "gk": jnp.tile(gk, ATTN_KV_HEADS).reshape(1, ATTN_KV_W),
        "log_gamma": jnp.stack([jax.nn.log_sigmoid(ret_decay_fwd.astype(F32)),
                                jax.nn.log_sigmoid(ret_decay_bwd.astype(F32))]),
        "ret_gain": ret_norm.reshape(1, RET_V_W),
        "w_a": w_branch_attn.astype(BF16),
        "w_r": w_branch_ret.astype(BF16),
        "w_o": w_out.astype(BF16),
        "g_ffn": norm_ffn.reshape(1, D_MODEL),
        "w_ffn_in": w_ffn_in.astype(BF16),
        "w_ffn_out": w_ffn_out.astype(BF16),
        "g_fin": norm_final.reshape(1, D_MODEL),
    }


@functools.lru_cache(maxsize=None)
def _rope_tables(seq_len):
    pos = np.arange(seq_len)
    row = (pos // GRID_W).astype(np.float64)[:, None]
    col = (pos % GRID_W).astype(np.float64)[:, None]

    def cos_sin(head_dim):
        n_freq = head_dim // 4
        inv_freq = ROPE_THETA ** (-np.arange(n_freq, dtype=np.float64) / n_freq)
        ang = np.concatenate([row * inv_freq, col * inv_freq], axis=-1)
        return np.cos(ang).astype(np.float32), np.sin(ang).astype(np.float32)

    ca, sa = cos_sin(ATTN_HEAD_DIM)
    cr, sr = cos_sin(RET_KEY_DIM)
    return (np.ascontiguousarray(np.concatenate([ca, sa], axis=-1).T),
            np.concatenate([ca, sa, ca, sa], axis=-1),
            np.concatenate([cr, sr], axis=-1))


def _trunk(x, p):
    batch, seq_len, _ = x.shape
    x2d = x.reshape(batch * seq_len, D_MODEL)
    qt, vt, k, qr, kr, vr, gr, gates = _in_proj(x2d, seq_len, p, _rope_tables(seq_len))
    attn_t = _attention(qt, k, vt, batch, seq_len)
    ret = _retention(qr, kr, vr, gr, p["log_gamma"], batch, seq_len)
    return _post(x2d, attn_t, ret, gates, p).reshape(batch, seq_len, D_MODEL)


def kernel(x_prompt, x_sample, norm_mix, w_in, b_gate, q_norm, k_norm, ret_decay_fwd, ret_decay_bwd,
           ret_norm, w_branch_attn, w_branch_ret, w_out, norm_ffn, w_ffn_in, w_ffn_out, norm_final):
    params = _prepare_params(norm_mix[0], w_in[0], b_gate[0], q_norm[0], k_norm[0], ret_decay_fwd[0],
                             ret_decay_bwd[0], ret_norm[0], w_branch_attn[0], w_branch_ret[0], w_out[0],
                             norm_ffn[0], w_ffn_in[0], w_ffn_out[0], norm_final)
    return _trunk(x_prompt, params), _trunk(x_sample, params)
```

```python
import functools
import math

import jax
import jax.numpy as jnp
import numpy as np
from jax import lax
from jax.experimental import pallas as pl
from jax.experimental.pallas import tpu as pltpu

F32 = jnp.float32
BF16 = jnp.bfloat16

D_MODEL = 1024
GRID_W = 64
ATTN_HEADS = 8
ATTN_KV_HEADS = 2
ATTN_GROUP = ATTN_HEADS // ATTN_KV_HEADS
ATTN_HEAD_DIM = 64
RET_HEADS = 4
RET_KEY_DIM = 128
RET_VALUE_DIM = 256
ATTN_Q_W = ATTN_HEADS * ATTN_HEAD_DIM
ATTN_KV_W = ATTN_KV_HEADS * ATTN_HEAD_DIM
RET_QK_W = RET_HEADS * RET_KEY_DIM
RET_V_W = RET_HEADS * RET_VALUE_DIM
D_FF = 2816
ROPE_THETA = 10000.0
EPS = 1e-6

TOKEN_TILE = 512
IN_PROJ_TILES = 2
ATTN_Q_TILES = 2
RET_CHUNK = 256
RET_CHUNKS_PER_STEP = 4
FFN_CHUNKS = ((0, 1536), (1536, 2816))
CHAIN_ROWS = 256
ONES_ROWS = 48
V7X_VMEM_LIMIT = 56 * 1024 * 1024

NT_DIMS = (((1,), (1,)), ((), ()))
TN_DIMS = (((0,), (0,)), ((), ()))


def _const_spec(shape):
    nd = len(shape)
    return pl.BlockSpec(shape, lambda *_: (0,) * nd, pipeline_mode=pl.Buffered(1))


def _rms(x):
    return x * lax.rsqrt(jnp.mean(x * x, axis=-1, keepdims=True) + EPS)


def _sigmoid(x):
    return 1.0 / (1.0 + jnp.exp(-x))


def _in_proj_kernel(x_ref, gmix_ref, wt_ref, ws_ref, wb_ref, bg_ref, gq_ref, gk_ref, rgain_ref,
                    tq_ref, tk_ref, tr_ref,
                    qt_ref, vt_ref, k_ref, qr_ref, kr_ref, vr_ref, gr_ref, gate_ref):
    rows = CHAIN_ROWS
    half = ATTN_HEAD_DIM // 2
    lane = lax.broadcasted_iota(jnp.int32, (rows, ATTN_KV_W), 1)
    lo = lane < ATTN_HEAD_DIM
    first_half = (lane % ATTN_HEAD_DIM) < half

    for rc in range(x_ref.shape[0] // rows):
        sl = slice(rc * rows, (rc + 1) * rows)
        tile, t0 = divmod(rc * rows, TOKEN_TILE)
        tl = slice(t0, t0 + rows)
        h = (_rms(x_ref[sl, :]) * gmix_ref[...]).astype(BF16)

        t = lax.dot_general(wt_ref[...], h, NT_DIMS, preferred_element_type=F32)
        cat, sat = tq_ref[:half, sl], tq_ref[half:, sl]
        gq = gq_ref[...]
        for hd in range(ATTN_HEADS):
            r0 = hd * ATTN_HEAD_DIM
            blk = t[r0:r0 + ATTN_HEAD_DIM]
            inv = lax.rsqrt(jnp.mean(blk * blk, axis=0, keepdims=True) + EPS)
            xn = blk * inv * gq
            x0, x1 = xn[:half], xn[half:]
            qt_ref[tile, r0:r0 + half, tl] = (x0 * cat - x1 * sat).astype(BF16)
            qt_ref[tile, r0 + half:r0 + ATTN_HEAD_DIM, tl] = (x0 * sat + x1 * cat).astype(BF16)
        vt_ref[tile, :, tl] = t[ATTN_Q_W:ATTN_Q_W + ATTN_KV_W].astype(BF16)

        o = 0
        kf = jnp.dot(h, ws_ref[:, o:o + ATTN_KV_W], preferred_element_type=F32)
        o += ATTN_KV_W
        k2 = kf * kf
        s_lo = jnp.sum(jnp.where(lo, k2, 0.0), axis=-1, keepdims=True)
        s_hi = jnp.sum(jnp.where(lo, 0.0, k2), axis=-1, keepdims=True)
        inv = jnp.where(lo, lax.rsqrt(s_lo * (1.0 / ATTN_HEAD_DIM) + EPS),
                        lax.rsqrt(s_hi * (1.0 / ATTN_HEAD_DIM) + EPS))
        kn = kf * inv * gk_ref[...]
        tab = tk_ref[sl, :]
        tab_rot = pltpu.roll(tab, half, 1)
        ck = jnp.where(first_half, tab, tab_rot)
        sk = jnp.where(first_half, -tab_rot, tab)
        partner = jnp.where(first_half, pltpu.roll(kn, ATTN_KV_W - half, 1), pltpu.roll(kn, half, 1))
        k_ref[sl, :] = (kn * ck + partner * sk).astype(BF16)

        tab = tr_ref[sl, :]
        tab_rot = pltpu.roll(tab, RET_KEY_DIM // 2, 1)
        cr = jnp.where(lo, tab, tab_rot)
        sr = jnp.where(lo, -tab_rot, tab)
        for dst, scale in ((qr_ref, RET_KEY_DIM ** -0.5), (kr_ref, None)):
            y = jnp.dot(h, ws_ref[:, o:o + RET_QK_W], preferred_element_type=F32)
            o += RET_QK_W
            for hd in range(RET_HEADS):
                c0 = hd * RET_KEY_DIM
                xh = y[:, c0:c0 + RET_KEY_DIM]
                r = xh * cr + pltpu.roll(xh, RET_KEY_DIM // 2, 1) * sr
                if scale is not None:
                    r = r * scale
                dst[sl, c0:c0 + RET_KEY_DIM] = r.astype(BF16)

        gl = jnp.dot(h, wb_ref[:, RET_V_W:2 * RET_V_W], preferred_element_type=F32)
        gr_ref[sl, :] = (gl * _sigmoid(gl) * rgain_ref[...]).astype(BF16)
        for c in range(2):
            c0 = 2 * RET_V_W + c * D_MODEL
            z = jnp.dot(h, wb_ref[:, c0:c0 + D_MODEL], preferred_element_type=F32)
            z = z + bg_ref[:, c * D_MODEL:(c + 1) * D_MODEL]
            gate_ref[sl, c * D_MODEL:(c + 1) * D_MODEL] = _sigmoid(z).astype(BF16)
        vr_ref[sl, :] = jnp.dot(h, wb_ref[:, :RET_V_W], preferred_element_type=F32).astype(BF16)


def _in_proj(x2d, seq_len, p, tables):
    n_tok = x2d.shape[0]
    tm = IN_PROJ_TILES * TOKEN_TILE
    n_tiles = n_tok // TOKEN_TILE
    tiles_per_seq = seq_len // tm

    def tok(w):
        return pl.BlockSpec((tm, w), lambda i: (i, 0))

    def pos_rows(w):
        return pl.BlockSpec((tm, w), lambda i: (i % tiles_per_seq, 0))

    def pos_lanes(r):
        return pl.BlockSpec((r, tm), lambda i: (0, i % tiles_per_seq))

    out_shape = (
        jax.ShapeDtypeStruct((n_tiles, ATTN_Q_W, TOKEN_TILE), BF16),
        jax.ShapeDtypeStruct((n_tiles, ATTN_KV_W, TOKEN_TILE), BF16),
        jax.ShapeDtypeStruct((n_tok, ATTN_KV_W), BF16),
        jax.ShapeDtypeStruct((n_tok, RET_QK_W), BF16),
        jax.ShapeDtypeStruct((n_tok, RET_QK_W), BF16),
        jax.ShapeDtypeStruct((n_tok, RET_V_W), BF16),
        jax.ShapeDtypeStruct((n_tok, RET_V_W), BF16),
        jax.ShapeDtypeStruct((n_tok, 2 * D_MODEL), BF16),
    )
    out_specs = (
        pl.BlockSpec((IN_PROJ_TILES, ATTN_Q_W, TOKEN_TILE), lambda i: (i, 0, 0)),
        pl.BlockSpec((IN_PROJ_TILES, ATTN_KV_W, TOKEN_TILE), lambda i: (i, 0, 0)),
        tok(ATTN_KV_W), tok(RET_QK_W), tok(RET_QK_W), tok(RET_V_W), tok(RET_V_W), tok(2 * D_MODEL),
    )
    in_specs = [
        tok(D_MODEL),
        _const_spec((1, D_MODEL)),
        _const_spec(p["w_t"].shape),
        _const_spec(p["w_s"].shape),
        _const_spec(p["w_b"].shape),
        _const_spec((1, 2 * D_MODEL)),
        _const_spec((ATTN_HEAD_DIM, CHAIN_ROWS)),
        _const_spec((1, ATTN_KV_W)),
        _const_spec((1, RET_V_W)),
        pos_lanes(ATTN_HEAD_DIM), pos_rows(ATTN_KV_W), pos_rows(RET_KEY_DIM),
    ]
    return pl.pallas_call(
        _in_proj_kernel,
        grid=(n_tok // tm,),
        in_specs=in_specs,
        out_specs=out_specs,
        out_shape=out_shape,
        compiler_params=pltpu.CompilerParams(
            dimension_semantics=("arbitrary",), vmem_limit_bytes=V7X_VMEM_LIMIT),
        name="in_proj",
    )(x2d, p["g_mix"], p["w_t"], p["w_s"], p["w_b"], p["b_gate"], p["gq"], p["gk"], p["ret_gain"], *tables)


def _attn_kernel(qt_ref, qnext_ref, k_ref, knext_ref, vt_ref, o_ref, qpad_ref, s_ref, mprev_ref, mcur_ref, acc_ref):
    n_tiles, tq = qt_ref.shape[0], qt_ref.shape[2]
    n_kv, tk = vt_ref.shape[1], vt_ref.shape[3]
    chains = [(t, hd) for t in range(n_tiles) for hd in range(ATTN_HEADS)]
    zeros = jnp.zeros((ATTN_HEAD_DIM, tq), BF16)
    ones = jnp.ones((ONES_ROWS, tk), BF16)
    neg_inf = jnp.full((1, tq), -jnp.inf, F32)

    def v_ext(j):
        return [jnp.concatenate([vt_ref[0, j, g * ATTN_HEAD_DIM:(g + 1) * ATTN_HEAD_DIM, :], ones], axis=0)
                for g in range(ATTN_KV_HEADS)]

    def load_query(c, ref):
        t, hd = chains[c]
        qh = ref[t, hd * ATTN_HEAD_DIM:(hd + 1) * ATTN_HEAD_DIM, :]
        qpad_ref[c] = jnp.concatenate([qh, zeros] if hd < ATTN_GROUP else [zeros, qh], axis=0)

    def probs(c):
        m_cur = mcur_ref[c]
        alpha = jnp.exp2(mprev_ref[c] - m_cur)
        return alpha, jnp.exp2(s_ref[c] - m_cur).astype(BF16), m_cur

    def stage_scores(c, kb, m_cur):
        s = jnp.dot(kb, qpad_ref[c], preferred_element_type=F32)
        s_ref[c] = s
        mprev_ref[c] = m_cur
        mcur_ref[c] = jnp.maximum(m_cur, jnp.max(s, axis=0, keepdims=True))

    def accumulate(c, alpha, pt, vext):
        pv = jnp.dot(vext[chains[c][1] // ATTN_GROUP], pt, preferred_element_type=F32)
        acc_ref[c] = alpha * acc_ref[c] + pv

    @pl.when((pl.program_id(0) == 0) & (pl.program_id(1) == 0))
    def _first_step():
        kb0 = k_ref[0, 0:tk, :]
        for c in range(len(chains)):
            load_query(c, qt_ref)
            stage_scores(c, kb0, neg_inf)

    acc_ref[...] = jnp.zeros(acc_ref.shape, F32)

    def body(j, carry):
        start = pl.multiple_of((j + 1) * tk, tk)
        kb_next = k_ref[0, pl.ds(start, tk), :]
        vext = v_ext(j)
        for c in range(len(chains)):
            alpha, pt, m_cur = probs(c)
            stage_scores(c, kb_next, m_cur)
            accumulate(c, alpha, pt, vext)
        return carry

    lax.fori_loop(0, n_kv - 1, body, 0)

    vext = v_ext(n_kv - 1)
    kb0 = knext_ref[0]
    for c, (t, hd) in enumerate(chains):
        alpha, pt, _ = probs(c)
        load_query(c, qnext_ref)
        stage_scores(c, kb0, neg_inf)
        accumulate(c, alpha, pt, vext)
        acc = acc_ref[c]
        denom = acc[ATTN_HEAD_DIM:ATTN_HEAD_DIM + 1, :]
        o_ref[t, hd * ATTN_HEAD_DIM:(hd + 1) * ATTN_HEAD_DIM, :] = (acc[:ATTN_HEAD_DIM] / denom).astype(BF16)


def _attention(qt, k, vt, batch, seq_len):
    tq = qt.shape[2]
    n_q = seq_len // tq
    n_tiles = ATTN_Q_TILES
    n_steps = n_q // n_tiles
    n_chains = n_tiles * ATTN_HEADS
    k3 = k.reshape(batch, seq_len, ATTN_KV_W)
    vt4 = vt.reshape(batch, n_q, ATTN_KV_W, tq)
    q_block = (n_tiles, ATTN_Q_W, tq)

    def next_step(b, i):
        return jnp.minimum(b * n_steps + i + 1, batch * n_steps - 1)

    return pl.pallas_call(
        _attn_kernel,
        grid=(batch, n_steps),
        in_specs=[
            pl.BlockSpec(q_block, lambda b, i: (b * n_steps + i, 0, 0)),
            pl.BlockSpec(q_block, lambda b, i: (next_step(b, i), 0, 0)),
            pl.BlockSpec((1, seq_len, ATTN_KV_W), lambda b, i: (b, 0, 0)),
            pl.BlockSpec((1, tq, ATTN_KV_W), lambda b, i: (next_step(b, i) // n_steps, 0, 0)),
            pl.BlockSpec((1, n_q, ATTN_KV_W, tq), lambda b, i: (b, 0, 0, 0)),
        ],
        out_specs=pl.BlockSpec(q_block, lambda b, i: (b * n_steps + i, 0, 0)),
        out_shape=jax.ShapeDtypeStruct(qt.shape, BF16),
        scratch_shapes=[
            pltpu.VMEM((n_chains, 2 * ATTN_HEAD_DIM, tq), BF16),
            pltpu.VMEM((n_chains, tq, tq), F32),
            pltpu.VMEM((n_chains, 1, tq), F32),
            pltpu.VMEM((n_chains, 1, tq), F32),
            pltpu.VMEM((n_chains, ATTN_HEAD_DIM + ONES_ROWS, tq), F32),
        ],
        compiler_params=pltpu.CompilerParams(
            dimension_semantics=("arbitrary", "arbitrary"), vmem_limit_bytes=V7X_VMEM_LIMIT),
        name="attn",
    )(qt, qt, k3, k3, vt4)


def _ret_kernel(lg_ref, q_ref, k_ref, v_ref, g_ref, o_ref,
                sf_ref, sb_ref, sball_ref, dmask_ref, qdf_ref, qdb_ref, kdf_ref, kdb_ref):
    b = pl.program_id(0)
    sweep = pl.program_id(1)
    i = pl.program_id(2)
    n_blk = pl.num_programs(2)
    C = RET_CHUNK
    n_sub = k_ref.shape[1] // C
    dk, dv = RET_KEY_DIM, RET_VALUE_DIM

    @pl.when((b == 0) & (sweep == 0) & (i == 0))
    def _decay_tables():
        diff = (lax.broadcasted_iota(jnp.int32, (C, C), 0)
                - lax.broadcasted_iota(jnp.int32, (C, C), 1)).astype(F32)
        row = lax.broadcasted_iota(jnp.int32, (C, dk), 0).astype(F32)
        for hd in range(RET_HEADS):
            lgf, lgb = lg_ref[0, hd], lg_ref[1, hd]
            dmask_ref[hd] = jnp.exp(jnp.where(diff >= 0, lgf * diff, -lgb * diff))
            qdf_ref[hd] = jnp.exp(lgf * (row + 1.0)).astype(BF16)
            qdb_ref[hd] = jnp.exp(lgb * (C - row)).astype(BF16)
            kdf_ref[hd] = jnp.exp(lgf * (C - 1.0 - row)).astype(BF16)
            kdb_ref[hd] = jnp.exp(lgb * row).astype(BF16)

    def decayed_kv(hd, rows, kdec_ref):
        kd = k_ref[0, rows, hd * dk:(hd + 1) * dk] * kdec_ref[hd]
        return lax.dot_general(kd, v_ref[0, rows, hd * dv:(hd + 1) * dv], TN_DIMS, preferred_element_type=F32)

    def chunk_decay(lg):
        return jnp.exp(jnp.full((1, dv), lg * C, F32))

    @pl.when(sweep == 0)
    def _right_to_left():
        @pl.when(i == 0)
        def _():
            sb_ref[...] = jnp.zeros_like(sb_ref)

        blk = n_blk - 1 - i
        for c in reversed(range(n_sub)):
            rows = slice(c * C, (c + 1) * C)
            for hd in range(RET_HEADS):
                state = sb_ref[hd]
                sball_ref[blk * n_sub + c, hd] = state.astype(BF16)
                sb_ref[hd] = chunk_decay(lg_ref[1, hd]) * state + decayed_kv(hd, rows, kdb_ref)

    @pl.when(sweep == 1)
    def _left_to_right():
        @pl.when(i == 0)
        def _():
            sf_ref[...] = jnp.zeros_like(sf_ref)

        for c in range(n_sub):
            rows = slice(c * C, (c + 1) * C)
            for hd in range(RET_HEADS):
                q = q_ref[0, rows, hd * dk:(hd + 1) * dk]
                k = k_ref[0, rows, hd * dk:(hd + 1) * dk]
                v = v_ref[0, rows, hd * dv:(hd + 1) * dv]
                state = sf_ref[hd]
                scores = lax.dot_general(q, k, NT_DIMS, preferred_element_type=F32) * dmask_ref[hd]
                lhs = jnp.concatenate(
                    [scores.astype(BF16), q * qdf_ref[hd], q * qdb_ref[hd]], axis=1)
                rhs = jnp.concatenate([v, state.astype(BF16), sball_ref[i * n_sub + c, hd]], axis=0)
                y = jnp.dot(lhs, rhs, preferred_element_type=F32)
                sf_ref[hd] = chunk_decay(lg_ref[0, hd]) * state + decayed_kv(hd, rows, kdf_ref)

                mean = jnp.mean(y, axis=-1, keepdims=True)
                d = y - mean
                var = jnp.mean(d * d, axis=-1, keepdims=True)
                yn = d * lax.rsqrt(var + EPS)
                gate = g_ref[0, rows, hd * dv:(hd + 1) * dv].astype(F32)
                o_ref[0, rows, hd * dv:(hd + 1) * dv] = (gate * yn).astype(BF16)


def _retention(qr, kr, vr, gr, log_gamma, batch, seq_len):
    tb = min(seq_len, RET_CHUNK * RET_CHUNKS_PER_STEP * (2 if seq_len <= 2048 else 1))
    n_blk = seq_len // tb
    n_chunks = seq_len // RET_CHUNK
    shp = lambda a: a.reshape(batch, seq_len, a.shape[-1])

    def both(b, p, i):
        return (b, jnp.where(p == 0, n_blk - 1 - i, i), 0)

    def fwd_only(b, p, i):
        return (b, jnp.where(p == 0, 0, i), 0)

    out = pl.pallas_call(
        _ret_kernel,
        grid=(batch, 2, n_blk),
        in_specs=[
            pl.BlockSpec(memory_space=pltpu.SMEM),
            pl.BlockSpec((1, tb, RET_QK_W), fwd_only),
            pl.BlockSpec((1, tb, RET_QK_W), both),
            pl.BlockSpec((1, tb, RET_V_W), both),
            pl.BlockSpec((1, tb, RET_V_W), fwd_only),
        ],
        out_specs=pl.BlockSpec((1, tb, RET_V_W), fwd_only),
        out_shape=jax.ShapeDtypeStruct((batch, seq_len, RET_V_W), BF16),
        scratch_shapes=[
            pltpu.VMEM((RET_HEADS, RET_KEY_DIM, RET_VALUE_DIM), F32),
            pltpu.VMEM((RET_HEADS, RET_KEY_DIM, RET_VALUE_DIM), F32),
            pltpu.VMEM((n_chunks, RET_HEADS, RET_KEY_DIM, RET_VALUE_DIM), BF16),
            pltpu.VMEM((RET_HEADS, RET_CHUNK, RET_CHUNK), F32),
            pltpu.VMEM((RET_HEADS, RET_CHUNK, RET_KEY_DIM), BF16),
            pltpu.VMEM((RET_HEADS, RET_CHUNK, RET_KEY_DIM), BF16),
            pltpu.VMEM((RET_HEADS, RET_CHUNK, RET_KEY_DIM), BF16),
            pltpu.VMEM((RET_HEADS, RET_CHUNK, RET_KEY_DIM), BF16),
        ],
        compiler_params=pltpu.CompilerParams(
            dimension_semantics=("arbitrary",) * 3, vmem_limit_bytes=V7X_VMEM_LIMIT),
        name="retention",
    )(log_gamma, shp(qr), shp(kr), shp(vr), shp(gr))
    return out.reshape(batch * seq_len, RET_V_W)


def _post_kernel(x_ref, at_ref, ret_ref, gate_ref, wa_ref, wr_ref, wo_ref,
                 gffn_ref, win_ref, wout_ref, gfin_ref, o_ref):
    rows = CHAIN_ROWS
    chains = [slice(r0, r0 + rows) for r0 in range(0, x_ref.shape[0], rows)]
    x1s = []
    for sl in chains:
        r = jnp.dot(ret_ref[sl, :], wr_ref[...], preferred_element_type=F32)
        a = lax.dot_general(at_ref[0, :, sl], wa_ref[...], TN_DIMS, preferred_element_type=F32)
        ga = gate_ref[sl, :D_MODEL].astype(F32)
        gr = gate_ref[sl, D_MODEL:].astype(F32)
        mixed = (ga * a + gr * r).astype(BF16)
        x1s.append(x_ref[sl, :] + jnp.dot(mixed, wo_ref[...], preferred_element_type=F32))
    for sl, x1 in zip(chains, x1s):
        h = (_rms(x1) * gffn_ref[...]).astype(BF16)
        acc = x1
        for c0, c1 in FFN_CHUNKS:
            gt = jnp.dot(h, win_ref[:, c0:c1], preferred_element_type=F32)
            up = jnp.dot(h, win_ref[:, D_FF + c0:D_FF + c1], preferred_element_type=F32)
            act = (gt * _sigmoid(gt) * up).astype(BF16)
            acc = acc + jnp.dot(act, wout_ref[c0:c1, :], preferred_element_type=F32)
        o_ref[sl, :] = _rms(acc) * gfin_ref[...]


def _post(x2d, attn_t, ret, gates, p):
    n_tok = x2d.shape[0]
    tm = TOKEN_TILE
    tok = lambda w: pl.BlockSpec((tm, w), lambda i: (i, 0))
    return pl.pallas_call(
        _post_kernel,
        grid=(n_tok // tm,),
        in_specs=[
            tok(D_MODEL),
            pl.BlockSpec((1, ATTN_Q_W, tm), lambda i: (i, 0, 0)),
            tok(RET_V_W), tok(2 * D_MODEL),
            _const_spec((ATTN_Q_W, D_MODEL)),
            _const_spec((RET_V_W, D_MODEL)),
            _const_spec((D_MODEL, D_MODEL)),
            _const_spec((1, D_MODEL)),
            _const_spec((D_MODEL, 2 * D_FF)),
            _const_spec((D_FF, D_MODEL)),
            _const_spec((1, D_MODEL)),
        ],
        out_specs=tok(D_MODEL),
        out_shape=jax.ShapeDtypeStruct((n_tok, D_MODEL), F32),
        compiler_params=pltpu.CompilerParams(
            dimension_semantics=("arbitrary",), vmem_limit_bytes=V7X_VMEM_LIMIT),
        name="post",
    )(x2d, attn_t, ret, gates, p["w_a"], p["w_r"], p["w_o"],
      p["g_ffn"], p["w_ffn_in"], p["w_ffn_out"], p["g_fin"])


def _deinterleave(w, n_heads, head_dim):
    w4 = w.reshape(w.shape[0], n_heads, head_dim // 2, 2)
    return jnp.concatenate([w4[..., 0], w4[..., 1]], axis=-1).reshape(w.shape[0], n_heads * head_dim)


def _prepare_params(norm_mix, w_in, b_gate, q_norm, k_norm, ret_decay_fwd, ret_decay_bwd, ret_norm,
                    w_branch_attn, w_branch_ret, w_out, norm_ffn, w_ffn_in, w_ffn_out, norm_final):
    widths = [ATTN_Q_W, ATTN_KV_W, ATTN_KV_W, RET_QK_W, RET_QK_W]
    offs = np.concatenate([[0], np.cumsum(widths)])
    w_qa, w_ka, w_va, w_qr, w_kr = [w_in[:, offs[j]:offs[j + 1]] for j in range(len(widths))]
    w_qa = _deinterleave(w_qa, ATTN_HEADS, ATTN_HEAD_DIM)
    w_ka = _deinterleave(w_ka, ATTN_KV_HEADS, ATTN_HEAD_DIM)
    w_qr = _deinterleave(w_qr, RET_HEADS, RET_KEY_DIM)
    w_kr = _deinterleave(w_kr, RET_HEADS, RET_KEY_DIM)
    q_scale = ATTN_HEAD_DIM ** -0.5 * math.log2(math.e)
    gq = _deinterleave(q_norm.reshape(1, -1), 1, ATTN_HEAD_DIM).reshape(-1) * q_scale
    gk = _deinterleave(k_norm.reshape(1, -1), 1, ATTN_HEAD_DIM).reshape(-1)
    return {
        "g_mix": norm_mix.reshape(1, D_MODEL),
        "w_t": jnp.concatenate([w_qa, w_va], axis=1).T.astype(BF16),
        "w_s": jnp.concatenate([w_ka, w_qr, w_kr], axis=1).astype(BF16),
        "w_b": w_in[:, offs[-1]:].astype(BF16),
        "b_gate": b_gate.reshape(1, 2 * D_MODEL),
        "gq": jnp.broadcast_to(gq.astype(F32)[:, None], (ATTN_HEAD_DIM, CHAIN_ROWS)),
        "gk": jnp.tile(gk, ATTN_KV_HEADS).reshape(1, ATTN_KV_W),
        "log_gamma": jnp.stack([jax.nn.log_sigmoid(ret_decay_fwd.astype(F32)),
                                jax.nn.log_sigmoid(ret_decay_bwd.astype(F32))]),
        "ret_gain": ret_norm.reshape(1, RET_V_W),
        "w_a": w_branch_attn.astype(BF16),
        "w_r": w_branch_ret.astype(BF16),
        "w_o": w_out.astype(BF16),
        "g_ffn": norm_ffn.reshape(1, D_MODEL),
        "w_ffn_in": w_ffn_in.astype(BF16),
        "w_ffn_out": w_ffn_out.astype(BF16),
        "g_fin": norm_final.reshape(1, D_MODEL),
    }


@functools.lru_cache(maxsize=None)
def _rope_tables(seq_len):
    pos = np.arange(seq_len)
    row = (pos // GRID_W).astype(np.float64)[:, None]
    col = (pos % GRID_W).astype(np.float64)[:, None]

    def cos_sin(head_dim):
        n_freq = head_dim // 4
        inv_freq = ROPE_THETA ** (-np.arange(n_freq, dtype=np.float64) / n_freq)
        ang = np.concatenate([row * inv_freq, col * inv_freq], axis=-1)
        return np.cos(ang).astype(np.float32), np.sin(ang).astype(np.float32)

    ca, sa = cos_sin(ATTN_HEAD_DIM)
    cr, sr = cos_sin(RET_KEY_DIM)
    return (np.ascontiguousarray(np.concatenate([ca, sa], axis=-1).T),
            np.concatenate([ca, sa, ca, sa], axis=-1),
            np.concatenate([cr, sr], axis=-1))


def _trunk(x, p):
    batch, seq_len, _ = x.shape
    x2d = x.reshape(batch * seq_len, D_MODEL)
    qt, vt, k, qr, kr, vr, gr, gates = _in_proj(x2d, seq_len, p, _rope_tables(seq_len))
    attn_t = _attention(qt, k, vt, batch, seq_len)
    ret = _retention(qr, kr, vr, gr, p["log_gamma"], batch, seq_len)
    return _post(x2d, attn_t, ret, gates, p).reshape(batch, seq_len, D_MODEL)


def kernel(x_prompt, x_sample, norm_mix, w_in, b_gate, q_norm, k_norm, ret_decay_fwd, ret_decay_bwd,
           ret_norm, w_branch_attn, w_branch_ret, w_out, norm_ffn, w_ffn_in, w_ffn_out, norm_final):
    params = _prepare_params(norm_mix[0], w_in[0], b_gate[0], q_norm[0], k_norm[0], ret_decay_fwd[0],
                             ret_decay_bwd[0], ret_norm[0], w_branch_attn[0], w_branch_ret[0], w_out[0],
                             norm_ffn[0], w_ffn_in[0], w_ffn_out[0], norm_final)
    return _trunk(x_prompt, params), _trunk(x_sample, params)
```

```python
import functools
import math

import jax
import jax.numpy as jnp
import numpy as np
from jax import lax
from jax.experimental import pallas as pl
from jax.experimental.pallas import tpu as pltpu

F32 = jnp.float32
BF16 = jnp.bfloat16

D_MODEL = 1024
GRID_W = 64
ATTN_HEADS = 8
ATTN_KV_HEADS = 2
ATTN_GROUP = ATTN_HEADS // ATTN_KV_HEADS
ATTN_HEAD_DIM = 64
RET_HEADS = 4
RET_KEY_DIM = 128
RET_VALUE_DIM = 256
ATTN_Q_W = ATTN_HEADS * ATTN_HEAD_DIM
ATTN_KV_W = ATTN_KV_HEADS * ATTN_HEAD_DIM
RET_QK_W = RET_HEADS * RET_KEY_DIM
RET_V_W = RET_HEADS * RET_VALUE_DIM
D_FF = 2816
ROPE_THETA = 10000.0
EPS = 1e-6

TOKEN_TILE = 512
IN_PROJ_TILES = 2
ATTN_Q_TILES = 2
RET_CHUNK = 256
RET_CHUNKS_PER_STEP = 4
FFN_CHUNKS = ((0, 1536), (1536, 2816))
CHAIN_ROWS = 256
ONES_ROWS = 48
V7X_VMEM_LIMIT = 56 * 1024 * 1024

NT_DIMS = (((1,), (1,)), ((), ()))
TN_DIMS = (((0,), (0,)), ((), ()))


def _const_spec(shape):
    nd = len(shape)
    return pl.BlockSpec(shape, lambda *_: (0,) * nd, pipeline_mode=pl.Buffered(1))


def _rms(x):
    return x * lax.rsqrt(jnp.mean(x * x, axis=-1, keepdims=True) + EPS)


def _sigmoid(x):
    return 1.0 / (1.0 + jnp.exp(-x))


def _in_proj_kernel(x_ref, gmix_ref, wt_ref, ws_ref, wb_ref, bg_ref, gq_ref, gk_ref, rgain_ref,
                    tq_ref, tk_ref, tr_ref,
                    qt_ref, vt_ref, k_ref, qr_ref, kr_ref, vr_ref, gr_ref, gate_ref):
    rows = CHAIN_ROWS
    half = ATTN_HEAD_DIM // 2
    lane = lax.broadcasted_iota(jnp.int32, (rows, ATTN_KV_W), 1)
    lo = lane < ATTN_HEAD_DIM
    first_half = (lane % ATTN_HEAD_DIM) < half

    for rc in range(x_ref.shape[0] // rows):
        sl = slice(rc * rows, (rc + 1) * rows)
        tile, t0 = divmod(rc * rows, TOKEN_TILE)
        tl = slice(t0, t0 + rows)
        h = (_rms(x_ref[sl, :]) * gmix_ref[...]).astype(BF16)

        t = lax.dot_general(wt_ref[...], h, NT_DIMS, preferred_element_type=F32)
        cat, sat = tq_ref[:half, sl], tq_ref[half:, sl]
        gq = gq_ref[...]
        for hd in range(ATTN_HEADS):
            r0 = hd * ATTN_HEAD_DIM
            blk = t[r0:r0 + ATTN_HEAD_DIM]
            inv = lax.rsqrt(jnp.mean(blk * blk, axis=0, keepdims=True) + EPS)
            xn = blk * inv * gq
            x0, x1 = xn[:half], xn[half:]
            qt_ref[tile, r0:r0 + half, tl] = (x0 * cat - x1 * sat).astype(BF16)
            qt_ref[tile, r0 + half:r0 + ATTN_HEAD_DIM, tl] = (x0 * sat + x1 * cat).astype(BF16)
        vt_ref[tile, :, tl] = t[ATTN_Q_W:ATTN_Q_W + ATTN_KV_W].astype(BF16)

        o = 0
        kf = jnp.dot(h, ws_ref[:, o:o + ATTN_KV_W], preferred_element_type=F32)
        o += ATTN_KV_W
        k2 = kf * kf
        s_lo = jnp.sum(jnp.where(lo, k2, 0.0), axis=-1, keepdims=True)
        s_hi = jnp.sum(jnp.where(lo, 0.0, k2), axis=-1, keepdims=True)
        inv = jnp.where(lo, lax.rsqrt(s_lo * (1.0 / ATTN_HEAD_DIM) + EPS),
                        lax.rsqrt(s_hi * (1.0 / ATTN_HEAD_DIM) + EPS))
        kn = kf * inv * gk_ref[...]
        tab = tk_ref[sl, :]
        tab_rot = pltpu.roll(tab, half, 1)
        ck = jnp.where(first_half, tab, tab_rot)
        sk = jnp.where(first_half, -tab_rot, tab)
        partner = jnp.where(first_half, pltpu.roll(kn, ATTN_KV_W - half, 1), pltpu.roll(kn, half, 1))
        k_ref[sl, :] = (kn * ck + partner * sk).astype(BF16)

        tab = tr_ref[sl, :]
        tab_rot = pltpu.roll(tab, RET_KEY_DIM // 2, 1)
        cr = jnp.where(lo, tab, tab_rot)
        sr = jnp.where(lo, -tab_rot, tab)
        for dst, scale in ((qr_ref, RET_KEY_DIM ** -0.5), (kr_ref, None)):
            y = jnp.dot(h, ws_ref[:, o:o + RET_QK_W], preferred_element_type=F32)
            o += RET_QK_W
            for hd in range(RET_HEADS):
                c0 = hd * RET_KEY_DIM
                xh = y[:, c0:c0 + RET_KEY_DIM]
                r = xh * cr + pltpu.roll(xh, RET_KEY_DIM // 2, 1) * sr
                if scale is not None:
                    r = r * scale
                dst[sl, c0:c0 + RET_KEY_DIM] = r.astype(BF16)

        gl = jnp.dot(h, wb_ref[:, RET_V_W:2 * RET_V_W], preferred_element_type=F32)
        gr_ref[sl, :] = (gl * _sigmoid(gl) * rgain_ref[...]).astype(BF16)
        for c in range(2):
            c0 = 2 * RET_V_W + c * D_MODEL
            z = jnp.dot(h, wb_ref[:, c0:c0 + D_MODEL], preferred_element_type=F32)
            z = z + bg_ref[:, c * D_MODEL:(c + 1) * D_MODEL]
            gate_ref[sl, c * D_MODEL:(c + 1) * D_MODEL] = _sigmoid(z).astype(BF16)
        vr_ref[sl, :] = jnp.dot(h, wb_ref[:, :RET_V_W], preferred_element_type=F32).astype(BF16)


def _in_proj(x2d, seq_len, p, tables):
    n_tok = x2d.shape[0]
    tm = IN_PROJ_TILES * TOKEN_TILE
    n_tiles = n_tok // TOKEN_TILE
    tiles_per_seq = seq_len // tm

    def tok(w):
        return pl.BlockSpec((tm, w), lambda i: (i, 0))

    def pos_rows(w):
        return pl.BlockSpec((tm, w), lambda i: (i % tiles_per_seq, 0))

    def pos_lanes(r):
        return pl.BlockSpec((r, tm), lambda i: (0, i % tiles_per_seq))

    out_shape = (
        jax.ShapeDtypeStruct((n_tiles, ATTN_Q_W, TOKEN_TILE), BF16),
        jax.ShapeDtypeStruct((n_tiles, ATTN_KV_W, TOKEN_TILE), BF16),
        jax.ShapeDtypeStruct((n_tok, ATTN_KV_W), BF16),
        jax.ShapeDtypeStruct((n_tok, RET_QK_W), BF16),
        jax.ShapeDtypeStruct((n_tok, RET_QK_W), BF16),
        jax.ShapeDtypeStruct((n_tok, RET_V_W), BF16),
        jax.ShapeDtypeStruct((n_tok, RET_V_W), BF16),
        jax.ShapeDtypeStruct((n_tok, 2 * D_MODEL), BF16),
    )
    out_specs = (
        pl.BlockSpec((IN_PROJ_TILES, ATTN_Q_W, TOKEN_TILE), lambda i: (i, 0, 0)),
        pl.BlockSpec((IN_PROJ_TILES, ATTN_KV_W, TOKEN_TILE), lambda i: (i, 0, 0)),
        tok(ATTN_KV_W), tok(RET_QK_W), tok(RET_QK_W), tok(RET_V_W), tok(RET_V_W), tok(2 * D_MODEL),
    )
    in_specs = [
        tok(D_MODEL),
        _const_spec((1, D_MODEL)),
        _const_spec(p["w_t"].shape),
        _const_spec(p["w_s"].shape),
        _const_spec(p["w_b"].shape),
        _const_spec((1, 2 * D_MODEL)),
        _const_spec((ATTN_HEAD_DIM, CHAIN_ROWS)),
        _const_spec((1, ATTN_KV_W)),
        _const_spec((1, RET_V_W)),
        pos_lanes(ATTN_HEAD_DIM), pos_rows(ATTN_KV_W), pos_rows(RET_KEY_DIM),
    ]
    return pl.pallas_call(
        _in_proj_kernel,
        grid=(n_tok // tm,),
        in_specs=in_specs,
        out_specs=out_specs,
        out_shape=out_shape,
        compiler_params=pltpu.CompilerParams(
            dimension_semantics=("arbitrary",), vmem_limit_bytes=V7X_VMEM_LIMIT),
        name="in_proj",
    )(x2d, p["g_mix"], p["w_t"], p["w_s"], p["w_b"], p["b_gate"], p["gq"], p["gk"], p["ret_gain"], *tables)


def _attn_kernel(qt_ref, qnext_ref, k_ref, knext_ref, vt_ref, o_ref, qpad_ref, s_ref, mprev_ref, mcur_ref, acc_ref):
    n_tiles, tq = qt_ref.shape[0], qt_ref.shape[2]
    n_kv, tk = vt_ref.shape[1], vt_ref.shape[3]
    chains = [(t, hd) for t in range(n_tiles) for hd in range(ATTN_HEADS)]
    zeros = jnp.zeros((ATTN_HEAD_DIM, tq), BF16)
    ones = jnp.ones((ONES_ROWS, tk), BF16)
    neg_inf = jnp.full((1, tq), -jnp.inf, F32)

    def v_ext(j):
        return [jnp.concatenate([vt_ref[0, j, g * ATTN_HEAD_DIM:(g + 1) * ATTN_HEAD_DIM, :], ones], axis=0)
                for g in range(ATTN_KV_HEADS)]

    def load_query(c, ref):
        t, hd = chains[c]
        qh = ref[t, hd * ATTN_HEAD_DIM:(hd + 1) * ATTN_HEAD_DIM, :]
        qpad_ref[c] = jnp.concatenate([qh, zeros] if hd < ATTN_GROUP else [zeros, qh], axis=0)

    def probs(c):
        m_cur = mcur_ref[c]
        alpha = jnp.exp2(mprev_ref[c] - m_cur)
        return alpha, jnp.exp2(s_ref[c] - m_cur).astype(BF16), m_cur

    def stage_scores(c, kb, m_cur):
        s = jnp.dot(kb, qpad_ref[c], preferred_element_type=F32)
        s_ref[c] = s
        mprev_ref[c] = m_cur
        mcur_ref[c] = jnp.maximum(m_cur, jnp.max(s, axis=0, keepdims=True))

    def accumulate(c, alpha, pt, vext):
        pv = jnp.dot(vext[chains[c][1] // ATTN_GROUP], pt, preferred_element_type=F32)
        acc_ref[c] = alpha * acc_ref[c] + pv

    @pl.when((pl.program_id(0) == 0) & (pl.program_id(1) == 0))
    def _first_step():
        kb0 = k_ref[0, 0:tk, :]
        for c in range(len(chains)):
            load_query(c, qt_ref)
            stage_scores(c, kb0, neg_inf)

    acc_ref[...] = jnp.zeros(acc_ref.shape, F32)

    def body(j, carry):
        start = pl.multiple_of((j + 1) * tk, tk)
        kb_next = k_ref[0, pl.ds(start, tk), :]
        vext = v_ext(j)
        for c in range(len(chains)):
            alpha, pt, m_cur = probs(c)
            stage_scores(c, kb_next, m_cur)
            accumulate(c, alpha, pt, vext)
        return carry

    lax.fori_loop(0, n_kv - 1, body, 0)

    vext = v_ext(n_kv - 1)
    kb0 = knext_ref[0]
    for c, (t, hd) in enumerate(chains):
        alpha, pt, _ = probs(c)
        load_query(c, qnext_ref)
        stage_scores(c, kb0, neg_inf)
        accumulate(c, alpha, pt, vext)
        acc = acc_ref[c]
        denom = acc[ATTN_HEAD_DIM:ATTN_HEAD_DIM + 1, :]
        o_ref[t, hd * ATTN_HEAD_DIM:(hd + 1) * ATTN_HEAD_DIM, :] = (acc[:ATTN_HEAD_DIM] / denom).astype(BF16)


def _attention(qt, k, vt, batch, seq_len):
    tq = qt.shape[2]
    n_q = seq_len // tq
    n_tiles = ATTN_Q_TILES
    n_steps = n_q // n_tiles
    n_chains = n_tiles * ATTN_HEADS
    k3 = k.reshape(batch, seq_len, ATTN_KV_W)
    vt4 = vt.reshape(batch, n_q, ATTN_KV_W, tq)
    q_block = (n_tiles, ATTN_Q_W, tq)

    def next_step(b, i):
        return jnp.minimum(b * n_steps + i + 1, batch * n_steps - 1)

    return pl.pallas_call(
        _attn_kernel,
        grid=(batch, n_steps),
        in_specs=[
            pl.BlockSpec(q_block, lambda b, i: (b * n_steps + i, 0, 0)),
            pl.BlockSpec(q_block, lambda b, i: (next_step(b, i), 0, 0)),
            pl.BlockSpec((1, seq_len, ATTN_KV_W), lambda b, i: (b, 0, 0)),
            pl.BlockSpec((1, tq, ATTN_KV_W), lambda b, i: (next_step(b, i) // n_steps, 0, 0)),
            pl.BlockSpec((1, n_q, ATTN_KV_W, tq), lambda b, i: (b, 0, 0, 0)),
        ],
        out_specs=pl.BlockSpec(q_block, lambda b, i: (b * n_steps + i, 0, 0)),
        out_shape=jax.ShapeDtypeStruct(qt.shape, BF16),
        scratch_shapes=[
            pltpu.VMEM((n_chains, 2 * ATTN_HEAD_DIM, tq), BF16),
            pltpu.VMEM((n_chains, tq, tq), F32),
            pltpu.VMEM((n_chains, 1, tq), F32),
            pltpu.VMEM((n_chains, 1, tq), F32),
            pltpu.VMEM((n_chains, ATTN_HEAD_DIM + ONES_ROWS, tq), F32),
        ],
        compiler_params=pltpu.CompilerParams(
            dimension_semantics=("arbitrary", "arbitrary"), vmem_limit_bytes=V7X_VMEM_LIMIT),
        name="attn",
    )(qt, qt, k3, k3, vt4)


def _ret_kernel(lg_ref, q_ref, k_ref, v_ref, g_ref, o_ref,
                sf_ref, sb_ref, sball_ref, dmask_ref, qdf_ref, qdb_ref, kdf_ref, kdb_ref):
    b = pl.program_id(0)
    sweep = pl.program_id(1)
    i = pl.program_id(2)
    n_blk = pl.num_programs(2)
    C = RET_CHUNK
    n_sub = k_ref.shape[1] // C
    dk, dv = RET_KEY_DIM, RET_VALUE_DIM

    @pl.when((b == 0) & (sweep == 0) & (i == 0))
    def _decay_tables():
        diff = (lax.broadcasted_iota(jnp.int32, (C, C), 0)
                - lax.broadcasted_iota(jnp.int32, (C, C), 1)).astype(F32)
        row = lax.broadcasted_iota(jnp.int32, (C, dk), 0).astype(F32)
        for hd in range(RET_HEADS):
            lgf, lgb = lg_ref[0, hd], lg_ref[1, hd]
            dmask_ref[hd] = jnp.exp(jnp.where(diff >= 0, lgf * diff, -lgb * diff))
            qdf_ref[hd] = jnp.exp(lgf * (row + 1.0)).astype(BF16)
            qdb_ref[hd] = jnp.exp(lgb * (C - row)).astype(BF16)
            kdf_ref[hd] = jnp.exp(lgf * (C - 1.0 - row)).astype(BF16)
            kdb_ref[hd] = jnp.exp(lgb * row).astype(BF16)

    def decayed_kv(hd, rows, kdec_ref):
        kd = k_ref[0, rows, hd * dk:(hd + 1) * dk] * kdec_ref[hd]
        return lax.dot_general(kd, v_ref[0, rows, hd * dv:(hd + 1) * dv], TN_DIMS, preferred_element_type=F32)

    def chunk_decay(lg):
        return jnp.exp(jnp.full((1, dv), lg * C, F32))

    @pl.when(sweep == 0)
    def _right_to_left():
        @pl.when(i == 0)
        def _():
            sb_ref[...] = jnp.zeros_like(sb_ref)

        blk = n_blk - 1 - i
        for c in reversed(range(n_sub)):
            rows = slice(c * C, (c + 1) * C)
            for hd in range(RET_HEADS):
                state = sb_ref[hd]
                sball_ref[blk * n_sub + c, hd] = state.astype(BF16)
                sb_ref[hd] = chunk_decay(lg_ref[1, hd]) * state + decayed_kv(hd, rows, kdb_ref)

    @pl.when(sweep == 1)
    def _left_to_right():
        @pl.when(i == 0)
        def _():
            sf_ref[...] = jnp.zeros_like(sf_ref)

        for c in range(n_sub):
            rows = slice(c * C, (c + 1) * C)
            for hd in range(RET_HEADS):
                q = q_ref[0, rows, hd * dk:(hd + 1) * dk]
                k = k_ref[0, rows, hd * dk:(hd + 1) * dk]
                v = v_ref[0, rows, hd * dv:(hd + 1) * dv]
                state = sf_ref[hd]
                scores = lax.dot_general(q, k, NT_DIMS, preferred_element_type=F32) * dmask_ref[hd]
                lhs = jnp.concatenate(
                    [scores.astype(BF16), q * qdf_ref[hd], q * qdb_ref[hd]], axis=1)
                rhs = jnp.concatenate([v, state.astype(BF16), sball_ref[i * n_sub + c, hd]], axis=0)
                y = jnp.dot(lhs, rhs, preferred_element_type=F32)
                sf_ref[hd] = chunk_decay(lg_ref[0, hd]) * state + decayed_kv(hd, rows, kdf_ref)

                mean = jnp.mean(y, axis=-1, keepdims=True)
                d = y - mean
                var = jnp.mean(d * d, axis=-1, keepdims=True)
                yn = d * lax.rsqrt(var + EPS)
                gate = g_ref[0, rows, hd * dv:(hd + 1) * dv].astype(F32)
                o_ref[0, rows, hd * dv:(hd + 1) * dv] = (gate * yn).astype(BF16)


def _retention(qr, kr, vr, gr, log_gamma, batch, seq_len):
    tb = min(seq_len, RET_CHUNK * RET_CHUNKS_PER_STEP * (2 if seq_len <= 2048 else 1))
    n_blk = seq_len // tb
    n_chunks = seq_len // RET_CHUNK
    shp = lambda a: a.reshape(batch, seq_len, a.shape[-1])

    def both(b, p, i):
        return (b, jnp.where(p == 0, n_blk - 1 - i, i), 0)

    def fwd_only(b, p, i):
        return (b, jnp.where(p == 0, 0, i), 0)

    out = pl.pallas_call(
        _ret_kernel,
        grid=(batch, 2, n_blk),
        in_specs=[
            pl.BlockSpec(memory_space=pltpu.SMEM),
            pl.BlockSpec((1, tb, RET_QK_W), fwd_only),
            pl.BlockSpec((1, tb, RET_QK_W), both),
            pl.BlockSpec((1, tb, RET_V_W), both),
            pl.BlockSpec((1, tb, RET_V_W), fwd_only),
        ],
        out_specs=pl.BlockSpec((1, tb, RET_V_W), fwd_only),
        out_shape=jax.ShapeDtypeStruct((batch, seq_len, RET_V_W), BF16),
        scratch_shapes=[
            pltpu.VMEM((RET_HEADS, RET_KEY_DIM, RET_VALUE_DIM), F32),
            pltpu.VMEM((RET_HEADS, RET_KEY_DIM, RET_VALUE_DIM), F32),
            pltpu.VMEM((n_chunks, RET_HEADS, RET_KEY_DIM, RET_VALUE_DIM), BF16),
            pltpu.VMEM((RET_HEADS, RET_CHUNK, RET_CHUNK), F32),
            pltpu.VMEM((RET_HEADS, RET_CHUNK, RET_KEY_DIM), BF16),
            pltpu.VMEM((RET_HEADS, RET_CHUNK, RET_KEY_DIM), BF16),
            pltpu.VMEM((RET_HEADS, RET_CHUNK, RET_KEY_DIM), BF16),
            pltpu.VMEM((RET_HEADS, RET_CHUNK, RET_KEY_DIM), BF16),
        ],
        compiler_params=pltpu.CompilerParams(
            dimension_semantics=("arbitrary",) * 3, vmem_limit_bytes=V7X_VMEM_LIMIT),
        name="retention",
    )(log_gamma, shp(qr), shp(kr), shp(vr), shp(gr))
    return out.reshape(batch * seq_len, RET_V_W)


def _post_kernel(x_ref, at_ref, ret_ref, gate_ref, wa_ref, wr_ref, wo_ref,
                 gffn_ref, win_ref, wout_ref, gfin_ref, o_ref):
    rows = CHAIN_ROWS
    chains = [slice(r0, r0 + rows) for r0 in range(0, x_ref.shape[0], rows)]
    x1s = []
    for sl in chains:
        r = jnp.dot(ret_ref[sl, :], wr_ref[...], preferred_element_type=F32)
        a = lax.dot_general(at_ref[0, :, sl], wa_ref[...], TN_DIMS, preferred_element_type=F32)
        ga = gate_ref[sl, :D_MODEL].astype(F32)
        gr = gate_ref[sl, D_MODEL:].astype(F32)
        mixed = (ga * a + gr * r).astype(BF16)
        x1s.append(x_ref[sl, :] + jnp.dot(mixed, wo_ref[...], preferred_element_type=F32))
    for sl, x1 in zip(chains, x1s):
        h = (_rms(x1) * gffn_ref[...]).astype(BF16)
        acc = x1
        for c0, c1 in FFN_CHUNKS:
            gu = jnp.dot(h, win_ref[:, 2 * c0:2 * c1], preferred_element_type=F32)
            gt, up = gu[:, :c1 - c0], gu[:, c1 - c0:]
            act = (gt * _sigmoid(gt) * up).astype(BF16)
            acc = acc + jnp.dot(act, wout_ref[c0:c1, :], preferred_element_type=F32)
        o_ref[sl, :] = _rms(acc) * gfin_ref[...]


def _post(x2d, attn_t, ret, gates, p):
    n_tok = x2d.shape[0]
    tm = TOKEN_TILE
    tok = lambda w: pl.BlockSpec((tm, w), lambda i: (i, 0))
    return pl.pallas_call(
        _post_kernel,
        grid=(n_tok // tm,),
        in_specs=[
            tok(D_MODEL),
            pl.BlockSpec((1, ATTN_Q_W, tm), lambda i: (i, 0, 0)),
            tok(RET_V_W), tok(2 * D_MODEL),
            _const_spec((ATTN_Q_W, D_MODEL)),
            _const_spec((RET_V_W, D_MODEL)),
            _const_spec((D_MODEL, D_MODEL)),
            _const_spec((1, D_MODEL)),
            _const_spec((D_MODEL, 2 * D_FF)),
            _const_spec((D_FF, D_MODEL)),
            _const_spec((1, D_MODEL)),
        ],
        out_specs=tok(D_MODEL),
        out_shape=jax.ShapeDtypeStruct((n_tok, D_MODEL), F32),
        compiler_params=pltpu.CompilerParams(
            dimension_semantics=("arbitrary",), vmem_limit_bytes=V7X_VMEM_LIMIT),
        name="post",
    )(x2d, attn_t, ret, gates, p["w_a"], p["w_r"], p["w_o"],
      p["g_ffn"], p["w_ffn_in"], p["w_ffn_out"], p["g_fin"])


def _deinterleave(w, n_heads, head_dim):
    w4 = w.reshape(w.shape[0], n_heads, head_dim // 2, 2)
    return jnp.concatenate([w4[..., 0], w4[..., 1]], axis=-1).reshape(w.shape[0], n_heads * head_dim)


def _prepare_params(norm_mix, w_in, b_gate, q_norm, k_norm, ret_decay_fwd, ret_decay_bwd, ret_norm,
                    w_branch_attn, w_branch_ret, w_out, norm_ffn, w_ffn_in, w_ffn_out, norm_final):
    widths = [ATTN_Q_W, ATTN_KV_W, ATTN_KV_W, RET_QK_W, RET_QK_W]
    offs = np.concatenate([[0], np.cumsum(widths)])
    w_qa, w_ka, w_va, w_qr, w_kr = [w_in[:, offs[j]:offs[j + 1]] for j in range(len(widths))]
    w_qa = _deinterleave(w_qa, ATTN_HEADS, ATTN_HEAD_DIM)
    w_ka = _deinterleave(w_ka, ATTN_KV_HEADS, ATTN_HEAD_DIM)
    w_qr = _deinterleave(w_qr, RET_HEADS, RET_KEY_DIM)
    w_kr = _deinterleave(w_kr, RET_HEADS, RET_KEY_DIM)
    q_scale = ATTN_HEAD_DIM ** -0.5 * math.log2(math.e)
    gq = _deinterleave(q_norm.reshape(1, -1), 1, ATTN_HEAD_DIM).reshape(-1) * q_scale
    gk = _deinterleave(k_norm.reshape(1, -1), 1, ATTN_HEAD_DIM).reshape(-1)
    return {
        "g_mix": norm_mix.reshape(1, D_MODEL),
        "w_t": jnp.concatenate([w_qa, w_va], axis=1).T.astype(BF16),
        "w_s": jnp.concatenate([w_ka, w_qr, w_kr], axis=1).astype(BF16),
        "w_b": w_in[:, offs[-1]:].astype(BF16),
        "b_gate": b_gate.reshape(1, 2 * D_MODEL),
        "gq": jnp.broadcast_to(gq.astype(F32)[:, None], (ATTN_HEAD_DIM, CHAIN_ROWS)),
        "gk": jnp.tile(gk, ATTN_KV_HEADS).reshape(1, ATTN_KV_W),
        "log_gamma": jnp.stack([jax.nn.log_sigmoid(ret_decay_fwd.astype(F32)),
                                jax.nn.log_sigmoid(ret_decay_bwd.astype(F32))]),
        "ret_gain": ret_norm.reshape(1, RET_V_W),
        "w_a": w_branch_attn.astype(BF16),
        "w_r": w_branch_ret.astype(BF16),
        "w_o": w_out.astype(BF16),
        "g_ffn": norm_ffn.reshape(1, D_MODEL),
        "w_ffn_in": jnp.concatenate(
            [w_ffn_in[:, o + c0:o + c1] for c0, c1 in FFN_CHUNKS for o in (0, D_FF)], axis=1).astype(BF16),
        "w_ffn_out": w_ffn_out.astype(BF16),
        "g_fin": norm_final.reshape(1, D_MODEL),
    }


@functools.lru_cache(maxsize=None)
def _rope_tables(seq_len):
    pos = np.arange(seq_len)
    row = (pos // GRID_W).astype(np.float64)[:, None]
    col = (pos % GRID_W).astype(np.float64)[:, None]

    def cos_sin(head_dim):
        n_freq = head_dim // 4
        inv_freq = ROPE_THETA ** (-np.arange(n_freq, dtype=np.float64) / n_freq)
        ang = np.concatenate([row * inv_freq, col * inv_freq], axis=-1)
        return np.cos(ang).astype(np.float32), np.sin(ang).astype(np.float32)

    ca, sa = cos_sin(ATTN_HEAD_DIM)
    cr, sr = cos_sin(RET_KEY_DIM)
    return (np.ascontiguousarray(np.concatenate([ca, sa], axis=-1).T),
            np.concatenate([ca, sa, ca, sa], axis=-1),
            np.concatenate([cr, sr], axis=-1))


def _trunk(x, p):
    batch, seq_len, _ = x.shape
    x2d = x.reshape(batch * seq_len, D_MODEL)
    qt, vt, k, qr, kr, vr, gr, gates = _in_proj(x2d, seq_len, p, _rope_tables(seq_len))
    attn_t = _attention(qt, k, vt, batch, seq_len)
    ret = _retention(qr, kr, vr, gr, p["log_gamma"], batch, seq_len)
    return _post(x2d, attn_t, ret, gates, p).reshape(batch, seq_len, D_MODEL)


def kernel(x_prompt, x_sample, norm_mix, w_in, b_gate, q_norm, k_norm, ret_decay_fwd, ret_decay_bwd,
           ret_norm, w_branch_attn, w_branch_ret, w_out, norm_ffn, w_ffn_in, w_ffn_out, norm_final):
    params = _prepare_params(norm_mix[0], w_in[0], b_gate[0], q_norm[0], k_norm[0], ret_decay_fwd[0],
                             ret_decay_bwd[0], ret_norm[0], w_branch_attn[0], w_branch_ret[0], w_out[0],
                             norm_ffn[0], w_ffn_in[0], w_ffn_out[0], norm_final)
    return _trunk(x_prompt, params), _trunk(x_sample, params)
```
